```python
import jax, jax.numpy as jnp
from jax import lax
import numpy as np

D_MODEL = 1024
BATCH = 4
SEQ = 8192
DEPTH = 4

D_A = D_MODEL // 2
D_B = D_MODEL // 2
D_IN_CONV = 2 * D_A + 3 * D_B
CONV_A_WIDTH = 31
CONV_B_WIDTH = 3
HEAD_DIM = 64
N_HEADS = D_MODEL // HEAD_DIM
BLOCK_Q = 128
D_FF = ((8 * D_MODEL // 3 + 127) // 128) * 128
FFN_CONV_WIDTH = 3
N_EVEN = (DEPTH + 1) // 2
N_ODD = DEPTH // 2
EPS = 1e-6

kernel_name = "hybrid_conformer_shortconv_stickbreaking"


def rmsnorm(x, g):
    xf = x.astype(jnp.float32)
    y = xf * lax.rsqrt(jnp.mean(xf * xf, axis=-1, keepdims=True) + EPS)
    return (y * g.astype(jnp.float32)).astype(x.dtype)


def layernorm(x, g, b):
    xf = x.astype(jnp.float32)
    mu = jnp.mean(xf, axis=-1, keepdims=True)
    xc = xf - mu
    y = xc * lax.rsqrt(jnp.mean(xc * xc, axis=-1, keepdims=True) + EPS)
    return (y * g.astype(jnp.float32) + b.astype(jnp.float32)).astype(x.dtype)


def causal_dwconv(x, w, b=None):
    K, C = w.shape
    y = lax.conv_general_dilated(
        x, w[:, None, :].astype(x.dtype), window_strides=(1,),
        padding=[(K - 1, 0)], dimension_numbers=("NWC", "WIO", "NWC"),
        feature_group_count=C)
    if b is not None:
        y = y + b.astype(x.dtype)
    return y


def conv_mixer(h, w_in, a_dw_w, a_dw_b, a_ln_g, a_ln_b, b_dw_w, w_out):
    p = h @ w_in
    a_val, a_gate, b_gb, b_gc, b_h = jnp.split(
        p, [D_A, 2 * D_A, 2 * D_A + D_B, 2 * D_A + 2 * D_B], axis=-1)
    a = causal_dwconv(a_val * jax.nn.sigmoid(a_gate), a_dw_w, a_dw_b)
    a = jax.nn.silu(layernorm(a, a_ln_g, a_ln_b))
    b = b_gb * causal_dwconv(b_gc * b_h, b_dw_w)
    return jnp.concatenate([a, b], axis=-1) @ w_out


def stick_breaking_attention(q, k, v):
    B, H, S, dh = q.shape
    nb = S // BLOCK_Q
    scale = dh ** -0.5
    qb = q.astype(jnp.float32).reshape(B, H, nb, BLOCK_Q, dh).transpose(2, 0, 1, 3, 4)
    kf = k.astype(jnp.float32)
    starts = jnp.arange(nb, dtype=jnp.int32) * BLOCK_Q
    kpos = jnp.arange(S, dtype=jnp.int32)

    def block(args):
        q_blk, start = args
        z = jnp.einsum("bhqd,bhsd->bhqs", q_blk, kf) * scale
        qpos = start + jnp.arange(BLOCK_Q, dtype=jnp.int32)
        valid = kpos[None, :] < qpos[:, None]
        log_keep = jnp.where(valid, jax.nn.log_sigmoid(-z), 0.0)
        rc = lax.cumsum(log_keep, axis=3, reverse=True)
        after = jnp.concatenate([rc[..., 1:], jnp.zeros_like(rc[..., :1])], axis=-1)
        weight = jnp.where(valid, jnp.exp(jax.nn.log_sigmoid(z) + after), 0.0)
        return jnp.einsum("bhqs,bhsd->bhqd", weight.astype(v.dtype), v)

    out = lax.map(block, (qb, starts))
    return out.transpose(1, 2, 0, 3, 4).reshape(B, H, S, dh)


def attn_mixer(h, w_qkv, q_g, k_g, w_o):
    B, S, _ = h.shape
    qkv = (h @ w_qkv).reshape(B, S, 3, N_HEADS, HEAD_DIM)
    q = rmsnorm(qkv[:, :, 0], q_g).transpose(0, 2, 1, 3)
    k = rmsnorm(qkv[:, :, 1], k_g).transpose(0, 2, 1, 3)
    v = qkv[:, :, 2].transpose(0, 2, 1, 3)
    o = stick_breaking_attention(q, k, v)
    return o.transpose(0, 2, 1, 3).reshape(B, S, N_HEADS * HEAD_DIM) @ w_o


def conv_ffn(h, w_up, dw_w, dw_b, w_down):
    u = causal_dwconv(h @ w_up, dw_w, dw_b)
    gate, val = jnp.split(u, 2, axis=-1)
    return (jax.nn.silu(gate) * val) @ w_down


def setup_inputs(seed: int = 0) -> dict:
    key = jax.random.key(seed)
    ks = jax.random.split(key, 18)
    f32 = jnp.float32

    def nrm(k, shape, scale):
        return jax.random.normal(k, shape, f32) * scale

    def gain(k, shape):
        return 1.0 + 0.02 * jax.random.normal(k, shape, f32)

    return {
        "x": nrm(ks[0], (BATCH, SEQ, D_MODEL), 1.0),
        "mix_norm_g": gain(ks[1], (DEPTH, D_MODEL)),
        "ffn_norm_g": gain(ks[2], (DEPTH, D_MODEL)),
        "conv_w_in": nrm(ks[3], (N_EVEN, D_MODEL, D_IN_CONV), D_MODEL ** -0.5),
        "conv_a_dw_w": nrm(ks[4], (N_EVEN, CONV_A_WIDTH, D_A), CONV_A_WIDTH ** -0.5),
        "conv_a_dw_b": nrm(ks[5], (N_EVEN, D_A), 0.02),
        "conv_a_ln_g": gain(ks[6], (N_EVEN, D_A)),
        "conv_a_ln_b": nrm(ks[7], (N_EVEN, D_A), 0.02),
        "conv_b_dw_w": nrm(ks[8], (N_EVEN, CONV_B_WIDTH, D_B), CONV_B_WIDTH ** -0.5),
        "conv_w_out": nrm(ks[9], (N_EVEN, D_A + D_B, D_MODEL), (D_A + D_B) ** -0.5),
        "attn_w_qkv": nrm(ks[10], (N_ODD, D_MODEL, 3 * N_HEADS * HEAD_DIM), D_MODEL ** -0.5),
        "attn_q_g": gain(ks[11], (N_ODD, HEAD_DIM)),
        "attn_k_g": gain(ks[12], (N_ODD, HEAD_DIM)),
        "attn_w_o": nrm(ks[13], (N_ODD, N_HEADS * HEAD_DIM, D_MODEL), (N_HEADS * HEAD_DIM) ** -0.5),
        "ffn_w_up": nrm(ks[14], (DEPTH, D_MODEL, 2 * D_FF), D_MODEL ** -0.5),
        "ffn_dw_w": nrm(ks[15], (DEPTH, FFN_CONV_WIDTH, 2 * D_FF), FFN_CONV_WIDTH ** -0.5),
        "ffn_dw_b": nrm(ks[16], (DEPTH, 2 * D_FF), 0.02),
        "ffn_w_down": nrm(ks[17], (DEPTH, D_FF, D_MODEL), D_FF ** -0.5),
    }


def reference(x, mix_norm_g, ffn_norm_g, conv_w_in, conv_a_dw_w, conv_a_dw_b,
              conv_a_ln_g, conv_a_ln_b, conv_b_dw_w, conv_w_out, attn_w_qkv,
              attn_q_g, attn_k_g, attn_w_o, ffn_w_up, ffn_dw_w, ffn_dw_b,
              ffn_w_down):
    for layer in range(DEPTH):
        i = layer // 2
        h = rmsnorm(x, mix_norm_g[layer])
        if layer % 2 == 0:
            x = x + conv_mixer(h, conv_w_in[i], conv_a_dw_w[i], conv_a_dw_b[i],
                               conv_a_ln_g[i], conv_a_ln_b[i], conv_b_dw_w[i],
                               conv_w_out[i])
        else:
            x = x + attn_mixer(h, attn_w_qkv[i], attn_q_g[i], attn_k_g[i], attn_w_o[i])
        h = rmsnorm(x, ffn_norm_g[layer])
        x = x + conv_ffn(h, ffn_w_up[layer], ffn_dw_w[layer], ffn_dw_b[layer], ffn_w_down[layer])
    return x
```

```python
import functools

import jax
import jax.numpy as jnp
from jax import lax
from jax.experimental import pallas as pl
from jax.experimental.pallas import tpu as pltpu

EPS = 1e-6
HEAD_DIM = 64
LANES = 128
SUBLANES = 8
VMEM_LIMIT = 56 * 1024 * 1024

BF16 = jnp.bfloat16
F32 = jnp.float32


def _resident(shape):
    zeros = (0,) * len(shape)
    return pl.BlockSpec(shape, lambda *_: zeros, pipeline_mode=pl.Buffered(1))


def _rmsnorm(x, g):
    return x * lax.rsqrt(jnp.mean(x * x, axis=-1, keepdims=True) + EPS) * g


def _dot(a, b):
    return jnp.dot(a, b, preferred_element_type=F32)


def _ffn_body(x_ref, g_ref, wg_ref, wv_ref, dwg_ref, dwv_ref, bg_ref, bv_ref, wd_ref,
              o_ref, h_ref, ext_g, ext_v, carry_g, carry_v, act_ref, *, tm, nc):
    @pl.when(pl.program_id(1) == 0)
    def _():
        carry_g[...] = jnp.zeros_like(carry_g)
        carry_v[...] = jnp.zeros_like(carry_v)

    x = x_ref[0]
    h_ref[...] = _rmsnorm(x, g_ref[...]).astype(BF16)

    def conv3(c, w_ref, dw_ref, b_ref, ext, carry):
        u = _dot(h_ref[...], w_ref[c])
        ext[0:SUBLANES] = carry[c]
        ext[SUBLANES:SUBLANES + tm] = u
        carry[c] = u[tm - SUBLANES:tm]
        dw = dw_ref[c]
        return (dw[2:3] * u + dw[1:2] * ext[SUBLANES - 1:SUBLANES - 1 + tm]
                + dw[0:1] * ext[SUBLANES - 2:SUBLANES - 2 + tm] + b_ref[c])

    def chunk(c, carry):
        gate = conv3(c, wg_ref, dwg_ref, bg_ref, ext_g, carry_g)
        val = conv3(c, wv_ref, dwv_ref, bv_ref, ext_v, carry_v)
        act_ref[c] = (gate * jax.nn.sigmoid(gate) * val).astype(BF16)
        return carry

    lax.fori_loop(0, nc, chunk, 0)

    acc = x
    for c in range(nc):
        acc = acc + _dot(act_ref[c], wd_ref[c])
    o_ref[0] = acc


def _ffn_call(x, g, wg, wv, dwg, dwv, bg, bv, wd, *, tm):
    B, S, D = x.shape
    nc, _, cw = wg.shape
    body = functools.partial(_ffn_body, tm=tm, nc=nc)
    tile = pl.BlockSpec((1, tm, D), lambda b, s: (b, s, 0))
    return pl.pallas_call(
        body,
        out_shape=jax.ShapeDtypeStruct(x.shape, F32),
        grid=(B, S // tm),
        in_specs=[tile, _resident(g.shape), _resident(wg.shape), _resident(wv.shape),
                  _resident(dwg.shape), _resident(dwv.shape), _resident(bg.shape),
                  _resident(bv.shape), _resident(wd.shape)],
        out_specs=tile,
        scratch_shapes=[
            pltpu.VMEM((tm, D), BF16),
            pltpu.VMEM((tm + SUBLANES, cw), F32),
            pltpu.VMEM((tm + SUBLANES, cw), F32),
            pltpu.VMEM((nc, SUBLANES, cw), F32),
            pltpu.VMEM((nc, SUBLANES, cw), F32),
            pltpu.VMEM((nc, tm, cw), BF16),
        ],
        compiler_params=pltpu.CompilerParams(
            dimension_semantics=("arbitrary", "arbitrary"), vmem_limit_bytes=VMEM_LIMIT),
        name="conv_ffn",
    )(x, g, wg, wv, dwg, dwv, bg, bv, wd)


A_HALO = 32
A_ROWS = 32


def _conv_body(x_ref, g_ref, win_ref, adw_ref, adb_ref, lng_ref, lnb_ref, bdw_ref, wout_ref,
               o_ref, h_ref, ext_a, ext_b, cat_ref, *, tm, da, ka):
    @pl.when(pl.program_id(1) == 0)
    def _():
        ext_a[0:A_HALO] = jnp.zeros((A_HALO, da), F32)
        ext_b[0:SUBLANES] = jnp.zeros((SUBLANES, da), F32)

    x = x_ref[0]
    h_ref[...] = _rmsnorm(x, g_ref[...]).astype(BF16)

    def proj(i):
        return _dot(h_ref[...], win_ref[:, i * da:(i + 1) * da])

    ext_a[A_HALO:A_HALO + tm] = proj(0) * jax.nn.sigmoid(proj(1))
    for r0 in range(0, tm, A_ROWS):
        acc = jnp.broadcast_to(adb_ref[...], (A_ROWS, da))
        for k in range(ka):
            start = A_HALO + r0 - (ka - 1) + k
            acc = acc + adw_ref[k:k + 1] * ext_a[start:start + A_ROWS]
        mu = jnp.mean(acc, axis=-1, keepdims=True)
        xc = acc - mu
        y = xc * lax.rsqrt(jnp.mean(xc * xc, axis=-1, keepdims=True) + EPS)
        y = y * lng_ref[...] + lnb_ref[...]
        cat_ref[r0:r0 + A_ROWS, 0:da] = (y * jax.nn.sigmoid(y)).astype(BF16)
    ext_a[0:A_HALO] = ext_a[tm:tm + A_HALO]

    m = proj(3) * proj(4)
    ext_b[SUBLANES:SUBLANES + tm] = m
    bdw = bdw_ref[...]
    y = (bdw[2:3] * m + bdw[1:2] * ext_b[SUBLANES - 1:SUBLANES - 1 + tm]
         + bdw[0:1] * ext_b[SUBLANES - 2:SUBLANES - 2 + tm])
    cat_ref[:, da:2 * da] = (proj(2) * y).astype(BF16)
    ext_b[0:SUBLANES] = m[tm - SUBLANES:tm]

    o_ref[0] = x + _dot(cat_ref[...], wout_ref[...])


def _conv_call(x, g, win, adw, adb, lng, lnb, bdw, wout, *, tm):
    B, S, D = x.shape
    ka, da = adw.shape
    assert ka - 1 <= A_HALO and tm % A_ROWS == 0
    body = functools.partial(_conv_body, tm=tm, da=da, ka=ka)
    tile = pl.BlockSpec((1, tm, D), lambda b, s: (b, s, 0))
    consts = (g, win, adw, adb, lng, lnb, bdw, wout)
    return pl.pallas_call(
        body,
        out_shape=jax.ShapeDtypeStruct(x.shape, F32),
        grid=(B, S // tm),
        in_specs=[tile] + [_resident(c.shape) for c in consts],
        out_specs=tile,
        scratch_shapes=[
            pltpu.VMEM((tm, D), BF16),
            pltpu.VMEM((tm + A_HALO, da), F32),
            pltpu.VMEM((tm + SUBLANES, da), F32),
            pltpu.VMEM((tm, 2 * da), BF16),
        ],
        compiler_params=pltpu.CompilerParams(
            dimension_semantics=("arbitrary", "arbitrary"), vmem_limit_bytes=VMEM_LIMIT),
        name="conv_mixer",
    )(x, *consts)


QKV_COLS = 256


def _qkv_body(x_ref, g_ref, w_ref, qg_ref, kg_ref, q_ref, k_ref, v_ref, h_ref, *, tm, d):
    x = x_ref[0]
    h_ref[...] = _rmsnorm(x, g_ref[...]).astype(BF16)
    low = lax.broadcasted_iota(jnp.int32, (tm, LANES), 1) < HEAD_DIM

    def head_norm(t, gain):
        t2 = t * t
        s_lo = jnp.sum(jnp.where(low, t2, 0.0), axis=-1, keepdims=True)
        s_hi = jnp.sum(jnp.where(low, 0.0, t2), axis=-1, keepdims=True)
        ms = jnp.where(low, s_lo, s_hi) * (1.0 / HEAD_DIM)
        return t * lax.rsqrt(ms + EPS) * gain

    for part, (out_ref, gain_ref) in enumerate(((q_ref, qg_ref), (k_ref, kg_ref), (v_ref, None))):
        for c0 in range(0, d, QKV_COLS):
            y = _dot(h_ref[...], w_ref[:, part * d + c0:part * d + c0 + QKV_COLS])
            for l0 in range(0, QKV_COLS, LANES):
                t = y[:, l0:l0 + LANES]
                if gain_ref is not None:
                    t = head_norm(t, gain_ref[...])
                out_ref[0, :, c0 + l0:c0 + l0 + LANES] = t.astype(BF16)


def _qkv_call(x, g, w, qg, kg, *, tm):
    B, S, D = x.shape
    d = w.shape[1] // 3
    body = functools.partial(_qkv_body, tm=tm, d=d)
    tile = pl.BlockSpec((1, tm, D), lambda b, s: (b, s, 0))
    otile = pl.BlockSpec((1, tm, d), lambda b, s: (b, s, 0))
    out = jax.ShapeDtypeStruct((B, S, d), BF16)
    return pl.pallas_call(
        body,
        out_shape=(out, out, out),
        grid=(B, S // tm),
        in_specs=[tile, _resident(g.shape), _resident(w.shape), _resident(qg.shape),
                  _resident(kg.shape)],
        out_specs=(otile, otile, otile),
        scratch_shapes=[pltpu.VMEM((tm, D), BF16)],
        compiler_params=pltpu.CompilerParams(
            dimension_semantics=("arbitrary", "arbitrary"), vmem_limit_bytes=VMEM_LIMIT),
        name="qkv_proj",
    )(x, g, w, qg, kg)


def _softplus(z):
    return jnp.maximum(z, 0.0) + jnp.log1p(jnp.exp(-jnp.abs(z)))


def _attn_body(q_ref, k_ref, v_ref, tri_ref, o_ref, run_ref, acc_ref, *, tq):
    qi = pl.program_id(2)
    q = q_ref[0]
    low = lax.broadcasted_iota(jnp.int32, (tq, LANES), 1) < HEAD_DIM
    q_heads = (jnp.where(low, q, jnp.zeros_like(q)), jnp.where(low, jnp.zeros_like(q), q))
    run_ref[...] = jnp.zeros_like(run_ref)
    acc_ref[...] = jnp.zeros_like(acc_ref)

    def block(kb, valid):
        start = pl.multiple_of(kb * tq, tq)
        kblk = k_ref[0, pl.ds(start, tq), :]
        vblk = v_ref[0, pl.ds(start, tq), :]
        for hd in range(2):
            z = lax.dot_general(q_heads[hd], kblk, (((1,), (1,)), ((), ())),
                                preferred_element_type=F32)
            sp = _softplus(z)
            if valid is not None:
                sp = jnp.where(valid, sp, 0.0)
            suffix = _dot(sp.astype(BF16), tri_ref[...])
            run = run_ref[hd]
            w = jnp.exp(z - suffix - jnp.concatenate([run] * (tq // LANES), axis=1))
            if valid is not None:
                w = jnp.where(valid, w, 0.0)
            acc_ref[hd] += _dot(w.astype(BF16), vblk)
            run_ref[hd] = run + jnp.broadcast_to(suffix[:, 0:1], (tq, LANES))

    row = lax.broadcasted_iota(jnp.int32, (tq, tq), 0)
    col = lax.broadcasted_iota(jnp.int32, (tq, tq), 1)
    block(qi, col < row)

    def earlier(i, carry):
        block(qi - 1 - i, None)
        return carry

    lax.fori_loop(0, qi, earlier, 0)
    o_ref[0] = jnp.where(low, acc_ref[0], acc_ref[1]).astype(BF16)


def _attn_call(q, k, v, tri, *, tq):
    B, S, d = q.shape
    body = functools.partial(_attn_body, tq=tq)
    qtile = pl.BlockSpec((1, tq, LANES), lambda b, h, i: (b, i, h))
    kvfull = pl.BlockSpec((1, S, LANES), lambda b, h, i: (b, 0, h))
    return pl.pallas_call(
        body,
        out_shape=jax.ShapeDtypeStruct((B, S, d), BF16),
        grid=(B, d // LANES, S // tq),
        in_specs=[qtile, kvfull, kvfull, _resident(tri.shape)],
        out_specs=qtile,
        scratch_shapes=[
            pltpu.VMEM((2, tq, LANES), F32),
            pltpu.VMEM((2, tq, LANES), F32),
        ],
        compiler_params=pltpu.CompilerParams(
            dimension_semantics=("arbitrary", "arbitrary", "arbitrary"),
            vmem_limit_bytes=VMEM_LIMIT),
        name="stickbreaking_attn",
    )(q, k, v, tri)


def _oproj_body(o_ref, w_ref, x_ref, out_ref):
    out_ref[0] = x_ref[0] + _dot(o_ref[0], w_ref[...])


def _oproj_call(o, w, x, *, tm):
    B, S, D = x.shape
    tile = pl.BlockSpec((1, tm, D), lambda b, s: (b, s, 0))
    otile = pl.BlockSpec((1, tm, o.shape[2]), lambda b, s: (b, s, 0))
    return pl.pallas_call(
        _oproj_body,
        out_shape=jax.ShapeDtypeStruct(x.shape, F32),
        grid=(B, S // tm),
        in_specs=[otile, _resident(w.shape), tile],
        out_specs=tile,
        compiler_params=pltpu.CompilerParams(
            dimension_semantics=("arbitrary", "arbitrary"), vmem_limit_bytes=VMEM_LIMIT),
        name="attn_out_proj",
    )(o, w, x)


FFN_CHUNK = 256


def _chunked_cols(a, cw):
    r, c = a.shape
    return a.reshape(r, c // cw, cw).transpose(1, 0, 2)


def kernel(x, mix_norm_g, ffn_norm_g, conv_w_in, conv_a_dw_w, conv_a_dw_b, conv_a_ln_g,
           conv_a_ln_b, conv_b_dw_w, conv_w_out, attn_w_qkv, attn_q_g, attn_k_g, attn_w_o,
           ffn_w_up, ffn_dw_w, ffn_dw_b, ffn_w_down):
    B, S, D = x.shape
    depth = mix_norm_g.shape[0]
    d_ff = ffn_w_down.shape[1]
    tm = min(512, S)
    tm_conv = min(256, S)
    tq = min(256, S)
    cw = FFN_CHUNK
    scale = HEAD_DIM ** -0.5

    tri = (lax.broadcasted_iota(jnp.int32, (tq, tq), 0)
           >= lax.broadcasted_iota(jnp.int32, (tq, tq), 1)).astype(BF16)

    for layer in range(depth):
        i = layer // 2
        g = mix_norm_g[layer][None, :]
        if layer % 2 == 0:
            x = _conv_call(
                x, g, conv_w_in[i].astype(BF16), conv_a_dw_w[i], conv_a_dw_b[i][None, :],
                conv_a_ln_g[i][None, :], conv_a_ln_b[i][None, :], conv_b_dw_w[i],
                conv_w_out[i].astype(BF16), tm=tm_conv)
        else:
            qg = jnp.tile(attn_q_g[i] * scale, LANES // HEAD_DIM)[None, :]
            kg = jnp.tile(attn_k_g[i], LANES // HEAD_DIM)[None, :]
            q, k, v = _qkv_call(x, g, attn_w_qkv[i].astype(BF16), qg, kg, tm=tm)
            o = _attn_call(q, k, v, tri, tq=tq)
            x = _oproj_call(o, attn_w_o[i].astype(BF16), x, tm=tm)

        w_up = ffn_w_up[layer].astype(BF16)
        dw = ffn_dw_w[layer]
        db = ffn_dw_b[layer][None, :]
        x = _ffn_call(
            x, ffn_norm_g[layer][None, :],
            _chunked_cols(w_up[:, :d_ff], cw), _chunked_cols(w_up[:, d_ff:], cw),
            _chunked_cols(dw[:, :d_ff], cw), _chunked_cols(dw[:, d_ff:], cw),
            _chunked_cols(db[:, :d_ff], cw), _chunked_cols(db[:, d_ff:], cw),
            ffn_w_down[layer].astype(BF16).reshape(d_ff // cw, cw, D), tm=tm)
    return x
```

```python
import functools

import jax
import jax.numpy as jnp
from jax import lax
from jax.experimental import pallas as pl
from jax.experimental.pallas import tpu as pltpu

EPS = 1e-6
HEAD_DIM = 64
LANES = 128
SUBLANES = 8
VMEM_LIMIT = 56 * 1024 * 1024

BF16 = jnp.bfloat16
F32 = jnp.float32


def _resident(shape):
    zeros = (0,) * len(shape)
    return pl.BlockSpec(shape, lambda *_: zeros, pipeline_mode=pl.Buffered(1))


def _rmsnorm(x, g):
    return x * lax.rsqrt(jnp.mean(x * x, axis=-1, keepdims=True) + EPS) * g


def _dot(a, b):
    return jnp.dot(a, b, preferred_element_type=F32)


def _ffn_body(x_ref, g_ref, wg_ref, wv_ref, dwg_ref, dwv_ref, bg_ref, bv_ref, wd_ref,
              o_ref, h_ref, ext_g, ext_v, carry_g, carry_v, act_ref, *, tm, nc):
    @pl.when(pl.program_id(1) == 0)
    def _():
        carry_g[...] = jnp.zeros_like(carry_g)
        carry_v[...] = jnp.zeros_like(carry_v)

    x = x_ref[0]
    h_ref[...] = _rmsnorm(x, g_ref[...]).astype(BF16)

    def conv3(c, w_ref, dw_ref, b_ref, ext, carry):
        u = _dot(h_ref[...], w_ref[c])
        ext[0:SUBLANES] = carry[c]
        ext[SUBLANES:SUBLANES + tm] = u
        carry[c] = u[tm - SUBLANES:tm]
        dw = dw_ref[c]
        return (dw[2:3] * u + dw[1:2] * ext[SUBLANES - 1:SUBLANES - 1 + tm]
                + dw[0:1] * ext[SUBLANES - 2:SUBLANES - 2 + tm] + b_ref[c])

    def chunk(c, carry):
        gate = conv3(c, wg_ref, dwg_ref, bg_ref, ext_g, carry_g)
        val = conv3(c, wv_ref, dwv_ref, bv_ref, ext_v, carry_v)
        act_ref[c] = (gate * jax.nn.sigmoid(gate) * val).astype(BF16)
        return carry

    lax.fori_loop(0, nc, chunk, 0)

    acc = x
    for c in range(nc):
        acc = acc + _dot(act_ref[c], wd_ref[c])
    o_ref[0] = acc


def _ffn_call(x, g, wg, wv, dwg, dwv, bg, bv, wd, *, tm):
    B, S, D = x.shape
    nc, _, cw = wg.shape
    body = functools.partial(_ffn_body, tm=tm, nc=nc)
    tile = pl.BlockSpec((1, tm, D), lambda b, s: (b, s, 0))
    return pl.pallas_call(
        body,
        out_shape=jax.ShapeDtypeStruct(x.shape, F32),
        grid=(B, S // tm),
        in_specs=[tile, _resident(g.shape), _resident(wg.shape), _resident(wv.shape),
                  _resident(dwg.shape), _resident(dwv.shape), _resident(bg.shape),
                  _resident(bv.shape), _resident(wd.shape)],
        out_specs=tile,
        scratch_shapes=[
            pltpu.VMEM((tm, D), BF16),
            pltpu.VMEM((tm + SUBLANES, cw), F32),
            pltpu.VMEM((tm + SUBLANES, cw), F32),
            pltpu.VMEM((nc, SUBLANES, cw), F32),
            pltpu.VMEM((nc, SUBLANES, cw), F32),
            pltpu.VMEM((nc, tm, cw), BF16),
        ],
        compiler_params=pltpu.CompilerParams(
            dimension_semantics=("arbitrary", "arbitrary"), vmem_limit_bytes=VMEM_LIMIT),
        name="conv_ffn",
    )(x, g, wg, wv, dwg, dwv, bg, bv, wd)


A_HALO = 32
A_ROWS = 32


def _conv_body(x_ref, g_ref, win_ref, adw_ref, adb_ref, lng_ref, lnb_ref, bdw_ref, wout_ref,
               o_ref, h_ref, ext_a, ext_b, cat_ref, *, tm, da, ka):
    @pl.when(pl.program_id(1) == 0)
    def _():
        ext_a[0:A_HALO] = jnp.zeros((A_HALO, da), F32)
        ext_b[0:SUBLANES] = jnp.zeros((SUBLANES, da), F32)

    x = x_ref[0]
    h_ref[...] = _rmsnorm(x, g_ref[...]).astype(BF16)

    def proj(i):
        return _dot(h_ref[...], win_ref[:, i * da:(i + 1) * da])

    ext_a[A_HALO:A_HALO + tm] = proj(0) * jax.nn.sigmoid(proj(1))
    for r0 in range(0, tm, A_ROWS):
        acc = jnp.broadcast_to(adb_ref[...], (A_ROWS, da))
        for k in range(ka):
            start = A_HALO + r0 - (ka - 1) + k
            acc = acc + adw_ref[k:k + 1] * ext_a[start:start + A_ROWS]
        mu = jnp.mean(acc, axis=-1, keepdims=True)
        xc = acc - mu
        y = xc * lax.rsqrt(jnp.mean(xc * xc, axis=-1, keepdims=True) + EPS)
        y = y * lng_ref[...] + lnb_ref[...]
        cat_ref[r0:r0 + A_ROWS, 0:da] = (y * jax.nn.sigmoid(y)).astype(BF16)
    ext_a[0:A_HALO] = ext_a[tm:tm + A_HALO]

    m = proj(3) * proj(4)
    ext_b[SUBLANES:SUBLANES + tm] = m
    bdw = bdw_ref[...]
    y = (bdw[2:3] * m + bdw[1:2] * ext_b[SUBLANES - 1:SUBLANES - 1 + tm]
         + bdw[0:1] * ext_b[SUBLANES - 2:SUBLANES - 2 + tm])
    cat_ref[:, da:2 * da] = (proj(2) * y).astype(BF16)
    ext_b[0:SUBLANES] = m[tm - SUBLANES:tm]

    o_ref[0] = x + _dot(cat_ref[...], wout_ref[...])


def _conv_call(x, g, win, adw, adb, lng, lnb, bdw, wout, *, tm):
    B, S, D = x.shape
    ka, da = adw.shape
    assert ka - 1 <= A_HALO and tm % A_ROWS == 0
    body = functools.partial(_conv_body, tm=tm, da=da, ka=ka)
    tile = pl.BlockSpec((1, tm, D), lambda b, s: (b, s, 0))
    consts = (g, win, adw, adb, lng, lnb, bdw, wout)
    return pl.pallas_call(
        body,
        out_shape=jax.ShapeDtypeStruct(x.shape, F32),
        grid=(B, S // tm),
        in_specs=[tile] + [_resident(c.shape) for c in consts],
        out_specs=tile,
        scratch_shapes=[
            pltpu.VMEM((tm, D), BF16),
            pltpu.VMEM((tm + A_HALO, da), F32),
            pltpu.VMEM((tm + SUBLANES, da), F32),
            pltpu.VMEM((tm, 2 * da), BF16),
        ],
        compiler_params=pltpu.CompilerParams(
            dimension_semantics=("arbitrary", "arbitrary"), vmem_limit_bytes=VMEM_LIMIT),
        name="conv_mixer",
    )(x, *consts)


QKV_COLS = 256


def _qkv_body(x_ref, g_ref, w_ref, qg_ref, kg_ref, q_ref, k_ref, v_ref, h_ref, *, tm, d):
    x = x_ref[0]
    h_ref[...] = _rmsnorm(x, g_ref[...]).astype(BF16)
    low = lax.broadcasted_iota(jnp.int32, (tm, LANES), 1) < HEAD_DIM

    def head_norm(t, gain):
        t2 = t * t
        s_lo = jnp.sum(jnp.where(low, t2, 0.0), axis=-1, keepdims=True)
        s_hi = jnp.sum(jnp.where(low, 0.0, t2), axis=-1, keepdims=True)
        ms = jnp.where(low, s_lo, s_hi) * (1.0 / HEAD_DIM)
        return t * lax.rsqrt(ms + EPS) * gain

    for part, (out_ref, gain_ref) in enumerate(((q_ref, qg_ref), (k_ref, kg_ref), (v_ref, None))):
        for c0 in range(0, d, QKV_COLS):
            y = _dot(h_ref[...], w_ref[:, part * d + c0:part * d + c0 + QKV_COLS])
            for l0 in range(0, QKV_COLS, LANES):
                t = y[:, l0:l0 + LANES]
                if gain_ref is not None:
                    t = head_norm(t, gain_ref[...])
                out_ref[0, :, c0 + l0:c0 + l0 + LANES] = t.astype(BF16)


def _qkv_call(x, g, w, qg, kg, *, tm):
    B, S, D = x.shape
    d = w.shape[1] // 3
    body = functools.partial(_qkv_body, tm=tm, d=d)
    tile = pl.BlockSpec((1, tm, D), lambda b, s: (b, s, 0))
    otile = pl.BlockSpec((1, tm, d), lambda b, s: (b, s, 0))
    out = jax.ShapeDtypeStruct((B, S, d), BF16)
    return pl.pallas_call(
        body,
        out_shape=(out, out, out),
        grid=(B, S // tm),
        in_specs=[tile, _resident(g.shape), _resident(w.shape), _resident(qg.shape),
                  _resident(kg.shape)],
        out_specs=(otile, otile, otile),
        scratch_shapes=[pltpu.VMEM((tm, D), BF16)],
        compiler_params=pltpu.CompilerParams(
            dimension_semantics=("arbitrary", "arbitrary"), vmem_limit_bytes=VMEM_LIMIT),
        name="qkv_proj",
    )(x, g, w, qg, kg)


LOG2E = 1.4426950408889634
RUN2_DONE = 160.0


def _softplus2(z2):
    neg_abs = lax.bitcast_convert_type(
        lax.bitcast_convert_type(z2, jnp.uint32) | jnp.uint32(0x80000000), F32)
    return jnp.maximum(z2, 0.0) + jnp.log2(1.0 + jnp.exp2(neg_abs))


def _attn_body(q_ref, k_ref, v_ref, tri_ref, o_ref, run_ref, acc_ref, *, tq):
    qi = pl.program_id(2)
    q = q_ref[0]
    low = lax.broadcasted_iota(jnp.int32, (tq, LANES), 1) < HEAD_DIM
    q_heads = (jnp.where(low, q, jnp.zeros_like(q)), jnp.where(low, jnp.zeros_like(q), q))
    run_ref[...] = jnp.zeros_like(run_ref)
    acc_ref[...] = jnp.zeros_like(acc_ref)

    def block(kb, valid):
        start = pl.multiple_of(kb * tq, tq)
        kblk = k_ref[0, pl.ds(start, tq), :]
        vblk = v_ref[0, pl.ds(start, tq), :]
        runs = []
        for hd in range(2):
            z2 = lax.dot_general(q_heads[hd], kblk, (((1,), (1,)), ((), ())),
                                 preferred_element_type=F32) * LOG2E
            sp = _softplus2(z2)
            if valid is not None:
                sp = jnp.where(valid, sp, 0.0)
            suffix = _dot(sp.astype(BF16), tri_ref[...])
            run = run_ref[hd]
            w = jnp.exp2(z2 - suffix - jnp.concatenate([run] * (tq // LANES), axis=1))
            if valid is not None:
                w = jnp.where(valid, w, 0.0)
            acc_ref[hd] += _dot(w.astype(BF16), vblk)
            runs.append(run + jnp.broadcast_to(suffix[:, 0:1], (tq, LANES)))
            run_ref[hd] = runs[hd]
        return jnp.min(jnp.minimum(runs[0], runs[1]))

    row = lax.broadcasted_iota(jnp.int32, (tq, tq), 0)
    col = lax.broadcasted_iota(jnp.int32, (tq, tq), 1)

    def more(state):
        i, min_run = state
        return jnp.logical_and(i < qi, min_run < RUN2_DONE)

    def earlier(state):
        i, _ = state
        return i + 1, block(qi - 1 - i, None)

    lax.while_loop(more, earlier, (jnp.int32(0), block(qi, col < row)))
    o_ref[0] = jnp.where(low, acc_ref[0], acc_ref[1]).astype(BF16)


def _attn_call(q, k, v, tri, *, tq):
    B, S, d = q.shape
    body = functools.partial(_attn_body, tq=tq)
    qtile = pl.BlockSpec((1, tq, LANES), lambda b, h, i: (b, i, h))
    kvfull = pl.BlockSpec((1, S, LANES), lambda b, h, i: (b, 0, h))
    return pl.pallas_call(
        body,
        out_shape=jax.ShapeDtypeStruct((B, S, d), BF16),
        grid=(B, d // LANES, S // tq),
        in_specs=[qtile, kvfull, kvfull, _resident(tri.shape)],
        out_specs=qtile,
        scratch_shapes=[
            pltpu.VMEM((2, tq, LANES), F32),
            pltpu.VMEM((2, tq, LANES), F32),
        ],
        compiler_params=pltpu.CompilerParams(
            dimension_semantics=("arbitrary", "arbitrary", "arbitrary"),
            vmem_limit_bytes=VMEM_LIMIT),
        name="stickbreaking_attn",
    )(q, k, v, tri)


def _oproj_body(o_ref, w_ref, x_ref, out_ref):
    out_ref[0] = x_ref[0] + _dot(o_ref[0], w_ref[...])


def _oproj_call(o, w, x, *, tm):
    B, S, D = x.shape
    tile = pl.BlockSpec((1, tm, D), lambda b, s: (b, s, 0))
    otile = pl.BlockSpec((1, tm, o.shape[2]), lambda b, s: (b, s, 0))
    return pl.pallas_call(
        _oproj_body,
        out_shape=jax.ShapeDtypeStruct(x.shape, F32),
        grid=(B, S // tm),
        in_specs=[otile, _resident(w.shape), tile],
        out_specs=tile,
        compiler_params=pltpu.CompilerParams(
            dimension_semantics=("arbitrary", "arbitrary"), vmem_limit_bytes=VMEM_LIMIT),
        name="attn_out_proj",
    )(o, w, x)


FFN_CHUNK = 256


def _chunked_cols(a, cw):
    r, c = a.shape
    return a.reshape(r, c // cw, cw).transpose(1, 0, 2)


def kernel(x, mix_norm_g, ffn_norm_g, conv_w_in, conv_a_dw_w, conv_a_dw_b, conv_a_ln_g,
           conv_a_ln_b, conv_b_dw_w, conv_w_out, attn_w_qkv, attn_q_g, attn_k_g, attn_w_o,
           ffn_w_up, ffn_dw_w, ffn_dw_b, ffn_w_down):
    B, S, D = x.shape
    depth = mix_norm_g.shape[0]
    d_ff = ffn_w_down.shape[1]
    tm = min(512, S)
    tm_conv = min(256, S)
    tq = min(256, S)
    cw = FFN_CHUNK
    scale = HEAD_DIM ** -0.5

    tri = (lax.broadcasted_iota(jnp.int32, (tq, tq), 0)
           >= lax.broadcasted_iota(jnp.int32, (tq, tq), 1)).astype(BF16)

    for layer in range(depth):
        i = layer // 2
        g = mix_norm_g[layer][None, :]
        if layer % 2 == 0:
            x = _conv_call(
                x, g, conv_w_in[i].astype(BF16), conv_a_dw_w[i], conv_a_dw_b[i][None, :],
                conv_a_ln_g[i][None, :], conv_a_ln_b[i][None, :], conv_b_dw_w[i],
                conv_w_out[i].astype(BF16), tm=tm_conv)
        else:
            qg = jnp.tile(attn_q_g[i] * scale, LANES // HEAD_DIM)[None, :]
            kg = jnp.tile(attn_k_g[i], LANES // HEAD_DIM)[None, :]
            q, k, v = _qkv_call(x, g, attn_w_qkv[i].astype(BF16), qg, kg, tm=tm)
            o = _attn_call(q, k, v, tri, tq=tq)
            x = _oproj_call(o, attn_w_o[i].astype(BF16), x, tm=tm)

        w_up = ffn_w_up[layer].astype(BF16)
        dw = ffn_dw_w[layer]
        db = ffn_dw_b[layer][None, :]
        x = _ffn_call(
            x, ffn_norm_g[layer][None, :],
            _chunked_cols(w_up[:, :d_ff], cw), _chunked_cols(w_up[:, d_ff:], cw),
            _chunked_cols(dw[:, :d_ff], cw), _chunked_cols(dw[:, d_ff:], cw),
            _chunked_cols(db[:, :d_ff], cw), _chunked_cols(db[:, d_ff:], cw),
            ffn_w_down[layer].astype(BF16).reshape(d_ff // cw, cw, D), tm=tm)
    return x
```

```python
import functools

import jax
import jax.numpy as jnp
from jax import lax
from jax.experimental import pallas as pl
from jax.experimental.pallas import tpu as pltpu

EPS = 1e-6
HEAD_DIM = 64
LANES = 128
SUBLANES = 8
VMEM_LIMIT = 56 * 1024 * 1024

BF16 = jnp.bfloat16
F32 = jnp.float32


def _resident(shape):
    zeros = (0,) * len(shape)
    return pl.BlockSpec(shape, lambda *_: zeros, pipeline_mode=pl.Buffered(1))


def _rmsnorm(x, g):
    return x * lax.rsqrt(jnp.mean(x * x, axis=-1, keepdims=True) + EPS) * g


def _dot(a, b):
    return jnp.dot(a, b, preferred_element_type=F32)


def _ffn_body(x_ref, g_ref, wg_ref, wv_ref, dwg_ref, dwv_ref, bg_ref, bv_ref, wd_ref,
              o_ref, h_ref, ext_g0, ext_v0, ext_g1, ext_v1, carry_g, carry_v, act_ref,
              *, tm, nc):
    assert nc % 2 == 1
    ext_g, ext_v = (ext_g0, ext_g1), (ext_v0, ext_v1)

    @pl.when(pl.program_id(1) == 0)
    def _():
        carry_g[...] = jnp.zeros_like(carry_g)
        carry_v[...] = jnp.zeros_like(carry_v)

    x = x_ref[0]
    h_ref[...] = _rmsnorm(x, g_ref[...]).astype(BF16)

    def up(c, slot):
        for w_ref, ext in ((wg_ref, ext_g[slot]), (wv_ref, ext_v[slot])):
            ext[SUBLANES:SUBLANES + tm] = _dot(h_ref[...], w_ref[c])

    def conv3(c, dw_ref, b_ref, ext, carry):
        ext[0:SUBLANES] = carry[c]
        carry[c] = ext[tm:tm + SUBLANES]
        dw = dw_ref[c]
        return (dw[2:3] * ext[SUBLANES:SUBLANES + tm]
                + dw[1:2] * ext[SUBLANES - 1:SUBLANES - 1 + tm]
                + dw[0:1] * ext[SUBLANES - 2:SUBLANES - 2 + tm] + b_ref[c])

    def activate(c, slot):
        gate = conv3(c, dwg_ref, bg_ref, ext_g[slot], carry_g)
        val = conv3(c, dwv_ref, bv_ref, ext_v[slot], carry_v)
        act_ref[c] = (gate * jax.nn.sigmoid(gate) * val).astype(BF16)

    def chunk_pair(j, carry):
        c = 2 * j
        activate(c, 0)
        up(c + 1, 1)
        activate(c + 1, 1)
        up(c + 2, 0)
        return carry

    up(0, 0)
    lax.fori_loop(0, (nc - 1) // 2, chunk_pair, 0)
    activate(nc - 1, 0)

    acc = x
    for c in range(nc):
        acc = acc + _dot(act_ref[c], wd_ref[c])
    o_ref[0] = acc


def _ffn_call(x, g, wg, wv, dwg, dwv, bg, bv, wd, *, tm):
    B, S, D = x.shape
    nc, _, cw = wg.shape
    body = functools.partial(_ffn_body, tm=tm, nc=nc)
    tile = pl.BlockSpec((1, tm, D), lambda b, s: (b, s, 0))
    return pl.pallas_call(
        body,
        out_shape=jax.ShapeDtypeStruct(x.shape, F32),
        grid=(B, S // tm),
        in_specs=[tile, _resident(g.shape), _resident(wg.shape), _resident(wv.shape),
                  _resident(dwg.shape), _resident(dwv.shape), _resident(bg.shape),
                  _resident(bv.shape), _resident(wd.shape)],
        out_specs=tile,
        scratch_shapes=[
            pltpu.VMEM((tm, D), BF16),
            pltpu.VMEM((tm + SUBLANES, cw), F32),
            pltpu.VMEM((tm + SUBLANES, cw), F32),
            pltpu.VMEM((tm + SUBLANES, cw), F32),
            pltpu.VMEM((tm + SUBLANES, cw), F32),
            pltpu.VMEM((nc, SUBLANES, cw), F32),
            pltpu.VMEM((nc, SUBLANES, cw), F32),
            pltpu.VMEM((nc, tm, cw), BF16),
        ],
        compiler_params=pltpu.CompilerParams(
            dimension_semantics=("arbitrary", "arbitrary"), vmem_limit_bytes=VMEM_LIMIT),
        name="conv_ffn",
    )(x, g, wg, wv, dwg, dwv, bg, bv, wd)


A_HALO = 32
A_ROWS = 32


def _conv_body(x_ref, g_ref, win_ref, adw_ref, adb_ref, lng_ref, lnb_ref, bdw_ref, wout_ref,
               o_ref, h_ref, ext_a, ext_b, cat_ref, *, tm, da, ka):
    @pl.when(pl.program_id(1) == 0)
    def _():
        ext_a[0:A_HALO] = jnp.zeros((A_HALO, da), F32)
        ext_b[0:SUBLANES] = jnp.zeros((SUBLANES, da), F32)

    x = x_ref[0]
    h_ref[...] = _rmsnorm(x, g_ref[...]).astype(BF16)

    def proj(i):
        return _dot(h_ref[...], win_ref[:, i * da:(i + 1) * da])

    ext_a[A_HALO:A_HALO + tm] = proj(0) * jax.nn.sigmoid(proj(1))
    for r0 in range(0, tm, A_ROWS):
        acc = jnp.broadcast_to(adb_ref[...], (A_ROWS, da))
        for k in range(ka):
            start = A_HALO + r0 - (ka - 1) + k
            acc = acc + adw_ref[k:k + 1] * ext_a[start:start + A_ROWS]
        mu = jnp.mean(acc, axis=-1, keepdims=True)
        xc = acc - mu
        y = xc * lax.rsqrt(jnp.mean(xc * xc, axis=-1, keepdims=True) + EPS)
        y = y * lng_ref[...] + lnb_ref[...]
        cat_ref[r0:r0 + A_ROWS, 0:da] = (y * jax.nn.sigmoid(y)).astype(BF16)
    ext_a[0:A_HALO] = ext_a[tm:tm + A_HALO]

    m = proj(3) * proj(4)
    ext_b[SUBLANES:SUBLANES + tm] = m
    bdw = bdw_ref[...]
    y = (bdw[2:3] * m + bdw[1:2] * ext_b[SUBLANES - 1:SUBLANES - 1 + tm]
         + bdw[0:1] * ext_b[SUBLANES - 2:SUBLANES - 2 + tm])
    cat_ref[:, da:2 * da] = (proj(2) * y).astype(BF16)
    ext_b[0:SUBLANES] = m[tm - SUBLANES:tm]

    o_ref[0] = x + _dot(cat_ref[...], wout_ref[...])


def _conv_call(x, g, win, adw, adb, lng, lnb, bdw, wout, *, tm):
    B, S, D = x.shape
    ka, da = adw.shape
    assert ka - 1 <= A_HALO and tm % A_ROWS == 0
    body = functools.partial(_conv_body, tm=tm, da=da, ka=ka)
    tile = pl.BlockSpec((1, tm, D), lambda b, s: (b, s, 0))
    consts = (g, win, adw, adb, lng, lnb, bdw, wout)
    return pl.pallas_call(
        body,
        out_shape=jax.ShapeDtypeStruct(x.shape, F32),
        grid=(B, S // tm),
        in_specs=[tile] + [_resident(c.shape) for c in consts],
        out_specs=tile,
        scratch_shapes=[
            pltpu.VMEM((tm, D), BF16),
            pltpu.VMEM((tm + A_HALO, da), F32),
            pltpu.VMEM((tm + SUBLANES, da), F32),
            pltpu.VMEM((tm, 2 * da), BF16),
        ],
        compiler_params=pltpu.CompilerParams(
            dimension_semantics=("arbitrary", "arbitrary"), vmem_limit_bytes=VMEM_LIMIT),
        name="conv_mixer",
    )(x, *consts)


QKV_COLS = 256


def _qkv_body(x_ref, g_ref, w_ref, qg_ref, kg_ref, q_ref, k_ref, v_ref, h_ref, *, tm, d):
    x = x_ref[0]
    h_ref[...] = _rmsnorm(x, g_ref[...]).astype(BF16)
    low = lax.broadcasted_iota(jnp.int32, (tm, LANES), 1) < HEAD_DIM

    def head_norm(t, gain):
        t2 = t * t
        s_lo = jnp.sum(jnp.where(low, t2, 0.0), axis=-1, keepdims=True)
        s_hi = jnp.sum(jnp.where(low, 0.0, t2), axis=-1, keepdims=True)
        ms = jnp.where(low, s_lo, s_hi) * (1.0 / HEAD_DIM)
        return t * lax.rsqrt(ms + EPS) * gain

    for part, (out_ref, gain_ref) in enumerate(((q_ref, qg_ref), (k_ref, kg_ref), (v_ref, None))):
        for c0 in range(0, d, QKV_COLS):
            y = _dot(h_ref[...], w_ref[:, part * d + c0:part * d + c0 + QKV_COLS])
            for l0 in range(0, QKV_COLS, LANES):
                t = y[:, l0:l0 + LANES]
                if gain_ref is not None:
                    t = head_norm(t, gain_ref[...])
                out_ref[0, :, c0 + l0:c0 + l0 + LANES] = t.astype(BF16)


def _qkv_call(x, g, w, qg, kg, *, tm):
    B, S, D = x.shape
    d = w.shape[1] // 3
    body = functools.partial(_qkv_body, tm=tm, d=d)
    tile = pl.BlockSpec((1, tm, D), lambda b, s: (b, s, 0))
    otile = pl.BlockSpec((1, tm, d), lambda b, s: (b, s, 0))
    out = jax.ShapeDtypeStruct((B, S, d), BF16)
    return pl.pallas_call(
        body,
        out_shape=(out, out, out),
        grid=(B, S // tm),
        in_specs=[tile, _resident(g.shape), _resident(w.shape), _resident(qg.shape),
                  _resident(kg.shape)],
        out_specs=(otile, otile, otile),
        scratch_shapes=[pltpu.VMEM((tm, D), BF16)],
        compiler_params=pltpu.CompilerParams(
            dimension_semantics=("arbitrary", "arbitrary"), vmem_limit_bytes=VMEM_LIMIT),
        name="qkv_proj",
    )(x, g, w, qg, kg)


LOG2E = 1.4426950408889634
RUN2_DONE = 160.0


def _softplus2(z2):
    neg_abs = lax.bitcast_convert_type(
        lax.bitcast_convert_type(z2, jnp.uint32) | jnp.uint32(0x80000000), F32)
    return jnp.maximum(z2, 0.0) + jnp.log2(1.0 + jnp.exp2(neg_abs))


def _attn_body(q_ref, k_ref, v_ref, tri_ref, o_ref, run_ref, acc_ref, *, tq, pairs):
    qi = pl.program_id(2)
    heads = range(2 * pairs)
    low = lax.broadcasted_iota(jnp.int32, (tq, LANES), 1) < HEAD_DIM

    def lane_tile(hd):
        return slice((hd // 2) * LANES, (hd // 2 + 1) * LANES)

    def q_head(hd):
        q = q_ref[0, :, lane_tile(hd)]
        zero = jnp.zeros_like(q)
        return jnp.where(low, q, zero) if hd % 2 == 0 else jnp.where(low, zero, q)

    q_heads = [q_head(hd) for hd in heads]

    def chain(hd, kb, valid, run):
        start = pl.multiple_of(kb * tq, tq)
        kblk = k_ref[0, pl.ds(start, tq), lane_tile(hd)]
        vblk = v_ref[0, pl.ds(start, tq), lane_tile(hd)]
        z2 = lax.dot_general(q_heads[hd], kblk, (((1,), (1,)), ((), ())),
                             preferred_element_type=F32) * LOG2E
        sp = _softplus2(z2)
        if valid is not None:
            sp = jnp.where(valid, sp, 0.0)
        suffix = _dot(sp.astype(BF16), tri_ref[...])
        arg = z2 - suffix
        if run is not None:
            arg = arg - jnp.concatenate([run] * (tq // LANES), axis=1)
        w = jnp.exp2(arg)
        if valid is not None:
            w = jnp.where(valid, w, 0.0)
        total = jnp.broadcast_to(suffix[:, 0:1], (tq, LANES))
        return _dot(w.astype(BF16), vblk), (total if run is None else run + total)

    def finish(accs):
        for p in range(pairs):
            o_ref[0, :, p * LANES:(p + 1) * LANES] = jnp.where(
                low, accs[2 * p], accs[2 * p + 1]).astype(BF16)

    row = lax.broadcasted_iota(jnp.int32, (tq, tq), 0)
    col = lax.broadcasted_iota(jnp.int32, (tq, tq), 1)
    below_diag = col < row

    @pl.when(qi == 0)
    def _():
        finish([chain(hd, 0, below_diag, None)[0] for hd in heads])

    @pl.when(qi > 0)
    def _():
        for hd in heads:
            pv_diag, run = chain(hd, qi, below_diag, None)
            pv_prev, run = chain(hd, qi - 1, None, run)
            acc_ref[hd] = pv_diag + pv_prev
            run_ref[hd] = run

        def more(state):
            i, min_run = state
            return jnp.logical_and(i < qi, min_run < RUN2_DONE)

        def earlier(state):
            i, _ = state
            for hd in heads:
                pv, run = chain(hd, qi - 1 - i, None, run_ref[hd])
                acc_ref[hd] += pv
                run_ref[hd] = run
            return i + 1, jnp.min(run_ref[...])

        lax.while_loop(more, earlier, (jnp.int32(1), jnp.min(run_ref[...])))
        finish([acc_ref[hd] for hd in heads])


ATTN_PAIRS = 2


def _attn_call(q, k, v, tri, *, tq):
    B, S, d = q.shape
    pairs = ATTN_PAIRS
    width = pairs * LANES
    body = functools.partial(_attn_body, tq=tq, pairs=pairs)
    qtile = pl.BlockSpec((1, tq, width), lambda b, h, i: (b, i, h))
    kvfull = pl.BlockSpec((1, S, width), lambda b, h, i: (b, 0, h))
    return pl.pallas_call(
        body,
        out_shape=jax.ShapeDtypeStruct((B, S, d), BF16),
        grid=(B, d // width, S // tq),
        in_specs=[qtile, kvfull, kvfull, _resident(tri.shape)],
        out_specs=qtile,
        scratch_shapes=[
            pltpu.VMEM((2 * pairs, tq, LANES), F32),
            pltpu.VMEM((2 * pairs, tq, LANES), F32),
        ],
        compiler_params=pltpu.CompilerParams(
            dimension_semantics=("arbitrary", "arbitrary", "arbitrary"),
            vmem_limit_bytes=VMEM_LIMIT),
        name="stickbreaking_attn",
    )(q, k, v, tri)


def _oproj_body(o_ref, w_ref, x_ref, out_ref):
    out_ref[0] = x_ref[0] + _dot(o_ref[0], w_ref[...])


def _oproj_call(o, w, x, *, tm):
    B, S, D = x.shape
    tile = pl.BlockSpec((1, tm, D), lambda b, s: (b, s, 0))
    otile = pl.BlockSpec((1, tm, o.shape[2]), lambda b, s: (b, s, 0))
    return pl.pallas_call(
        _oproj_body,
        out_shape=jax.ShapeDtypeStruct(x.shape, F32),
        grid=(B, S // tm),
        in_specs=[otile, _resident(w.shape), tile],
        out_specs=tile,
        compiler_params=pltpu.CompilerParams(
            dimension_semantics=("arbitrary", "arbitrary"), vmem_limit_bytes=VMEM_LIMIT),
        name="attn_out_proj",
    )(o, w, x)


FFN_CHUNK = 256


def _chunked_cols(a, cw):
    r, c = a.shape
    return a.reshape(r, c // cw, cw).transpose(1, 0, 2)


def kernel(x, mix_norm_g, ffn_norm_g, conv_w_in, conv_a_dw_w, conv_a_dw_b, conv_a_ln_g,
           conv_a_ln_b, conv_b_dw_w, conv_w_out, attn_w_qkv, attn_q_g, attn_k_g, attn_w_o,
           ffn_w_up, ffn_dw_w, ffn_dw_b, ffn_w_down):
    B, S, D = x.shape
    depth = mix_norm_g.shape[0]
    d_ff = ffn_w_down.shape[1]
    tm = min(512, S)
    tm_conv = min(256, S)
    tq = min(256, S)
    cw = FFN_CHUNK
    scale = HEAD_DIM ** -0.5

    tri = (lax.broadcasted_iota(jnp.int32, (tq, tq), 0)
           >= lax.broadcasted_iota(jnp.int32, (tq, tq), 1)).astype(BF16)

    for layer in range(depth):
        i = layer // 2
        g = mix_norm_g[layer][None, :]
        if layer % 2 == 0:
            x = _conv_call(
                x, g, conv_w_in[i].astype(BF16), conv_a_dw_w[i], conv_a_dw_b[i][None, :],
                conv_a_ln_g[i][None, :], conv_a_ln_b[i][None, :], conv_b_dw_w[i],
                conv_w_out[i].astype(BF16), tm=tm_conv)
        else:
            qg = jnp.tile(attn_q_g[i] * scale, LANES // HEAD_DIM)[None, :]
            kg = jnp.tile(attn_k_g[i], LANES // HEAD_DIM)[None, :]
            q, k, v = _qkv_call(x, g, attn_w_qkv[i].astype(BF16), qg, kg, tm=tm)
            o = _attn_call(q, k, v, tri, tq=tq)
            x = _oproj_call(o, attn_w_o[i].astype(BF16), x, tm=tm)

        w_up = ffn_w_up[layer].astype(BF16)
        dw = ffn_dw_w[layer]
        db = ffn_dw_b[layer][None, :]
        x = _ffn_call(
            x, ffn_norm_g[layer][None, :],
            _chunked_cols(w_up[:, :d_ff], cw), _chunked_cols(w_up[:, d_ff:], cw),
            _chunked_cols(dw[:, :d_ff], cw), _chunked_cols(dw[:, d_ff:], cw),
            _chunked_cols(db[:, :d_ff], cw), _chunked_cols(db[:, d_ff:], cw),
            ffn_w_down[layer].astype(BF16).reshape(d_ff // cw, cw, D), tm=tm)
    return x
```

```python
import functools

import jax
import jax.numpy as jnp
from jax import lax
from jax.experimental import pallas as pl
from jax.experimental.pallas import tpu as pltpu

EPS = 1e-6
HEAD_DIM = 64
LANES = 128
SUBLANES = 8
VMEM_LIMIT = 56 * 1024 * 1024

BF16 = jnp.bfloat16
F32 = jnp.float32


def _resident(shape):
    zeros = (0,) * len(shape)
    return pl.BlockSpec(shape, lambda *_: zeros, pipeline_mode=pl.Buffered(1))


def _rmsnorm(x, g):
    return x * lax.rsqrt(jnp.mean(x * x, axis=-1, keepdims=True) + EPS) * g


def _dot(a, b):
    return jnp.dot(a, b, preferred_element_type=F32)


def _ffn_body(x_ref, g_ref, wg_ref, wv_ref, dwg_ref, dwv_ref, bg_ref, bv_ref, wd_ref,
              o_ref, h_ref, ext_g0, ext_v0, ext_g1, ext_v1, carry_g, carry_v, act_ref,
              *, tm, nc):
    assert nc % 2 == 1
    ext_g, ext_v = (ext_g0, ext_g1), (ext_v0, ext_v1)

    @pl.when(pl.program_id(1) == 0)
    def _():
        carry_g[...] = jnp.zeros_like(carry_g)
        carry_v[...] = jnp.zeros_like(carry_v)

    x = x_ref[0]
    h_ref[...] = _rmsnorm(x, g_ref[...]).astype(BF16)

    def up(c, slot):
        for w_ref, ext in ((wg_ref, ext_g[slot]), (wv_ref, ext_v[slot])):
            ext[SUBLANES:SUBLANES + tm] = _dot(h_ref[...], w_ref[c])

    def conv3(c, dw_ref, b_ref, ext, carry):
        ext[0:SUBLANES] = carry[c]
        carry[c] = ext[tm:tm + SUBLANES]
        dw = dw_ref[c]
        return (dw[2:3] * ext[SUBLANES:SUBLANES + tm]
                + dw[1:2] * ext[SUBLANES - 1:SUBLANES - 1 + tm]
                + dw[0:1] * ext[SUBLANES - 2:SUBLANES - 2 + tm] + b_ref[c])

    def activate(c, slot):
        gate = conv3(c, dwg_ref, bg_ref, ext_g[slot], carry_g)
        val = conv3(c, dwv_ref, bv_ref, ext_v[slot], carry_v)
        act_ref[c] = (gate * jax.nn.sigmoid(gate) * val).astype(BF16)

    def chunk_pair(j, carry):
        c = 2 * j
        activate(c, 0)
        up(c + 1, 1)
        activate(c + 1, 1)
        up(c + 2, 0)
        return carry

    up(0, 0)
    lax.fori_loop(0, (nc - 1) // 2, chunk_pair, 0)
    activate(nc - 1, 0)

    acc = x
    for c in range(nc):
        acc = acc + _dot(act_ref[c], wd_ref[c])
    o_ref[0] = acc


def _ffn_call(x, g, wg, wv, dwg, dwv, bg, bv, wd, *, tm):
    B, S, D = x.shape
    nc, _, cw = wg.shape
    body = functools.partial(_ffn_body, tm=tm, nc=nc)
    tile = pl.BlockSpec((1, tm, D), lambda b, s: (b, s, 0))
    return pl.pallas_call(
        body,
        out_shape=jax.ShapeDtypeStruct(x.shape, F32),
        grid=(B, S // tm),
        in_specs=[tile, _resident(g.shape), _resident(wg.shape), _resident(wv.shape),
                  _resident(dwg.shape), _resident(dwv.shape), _resident(bg.shape),
                  _resident(bv.shape), _resident(wd.shape)],
        out_specs=tile,
        scratch_shapes=[
            pltpu.VMEM((tm, D), BF16),
            pltpu.VMEM((tm + SUBLANES, cw), F32),
            pltpu.VMEM((tm + SUBLANES, cw), F32),
            pltpu.VMEM((tm + SUBLANES, cw), F32),
            pltpu.VMEM((tm + SUBLANES, cw), F32),
            pltpu.VMEM((nc, SUBLANES, cw), F32),
            pltpu.VMEM((nc, SUBLANES, cw), F32),
            pltpu.VMEM((nc, tm, cw), BF16),
        ],
        compiler_params=pltpu.CompilerParams(
            dimension_semantics=("arbitrary", "arbitrary"), vmem_limit_bytes=VMEM_LIMIT),
        name="conv_ffn",
    )(x, g, wg, wv, dwg, dwv, bg, bv, wd)


A_HALO = 32
A_ROWS = 64
LN_ROWS = 32


def _conv_body(x_ref, g_ref, win_ref, adw_ref, adb_ref, lng_ref, lnb_ref, bdw_ref, wout_ref,
               o_ref, h_ref, ext_a, ext_b, conv_ref, cat_ref, *, tm, da, ka):
    @pl.when(pl.program_id(1) == 0)
    def _():
        ext_a[0:A_HALO] = jnp.zeros((A_HALO, da), F32)
        ext_b[0:SUBLANES] = jnp.zeros((SUBLANES, da), F32)

    x = x_ref[0]
    h_ref[...] = _rmsnorm(x, g_ref[...]).astype(BF16)

    def proj(i):
        return _dot(h_ref[...], win_ref[:, i * da:(i + 1) * da])

    ext_a[A_HALO:A_HALO + tm] = proj(0) * jax.nn.sigmoid(proj(1))
    base = A_HALO - (ka - 1)
    for r0 in range(0, tm, A_ROWS):
        for l0 in range(0, da, LANES):
            lanes = slice(l0, l0 + LANES)
            acc = jnp.broadcast_to(adb_ref[:, lanes], (A_ROWS, LANES))
            for phase in range(SUBLANES):
                rows = A_ROWS if phase == 0 else A_ROWS + SUBLANES
                part = None
                for k in range(ka):
                    if (base + k) % SUBLANES != phase:
                        continue
                    start = r0 + (base + k) - phase
                    term = adw_ref[k:k + 1, lanes] * ext_a[start:start + rows, lanes]
                    part = term if part is None else part + term
                acc = acc + part[phase:phase + A_ROWS]
            conv_ref[r0:r0 + A_ROWS, lanes] = acc
    for r0 in range(0, tm, LN_ROWS):
        acc = conv_ref[r0:r0 + LN_ROWS]
        mu = jnp.mean(acc, axis=-1, keepdims=True)
        xc = acc - mu
        y = xc * lax.rsqrt(jnp.mean(xc * xc, axis=-1, keepdims=True) + EPS)
        y = y * lng_ref[...] + lnb_ref[...]
        cat_ref[r0:r0 + LN_ROWS, 0:da] = (y * jax.nn.sigmoid(y)).astype(BF16)
    ext_a[0:A_HALO] = ext_a[tm:tm + A_HALO]

    m = proj(3) * proj(4)
    ext_b[SUBLANES:SUBLANES + tm] = m
    bdw = bdw_ref[...]
    y = (bdw[2:3] * m + bdw[1:2] * ext_b[SUBLANES - 1:SUBLANES - 1 + tm]
         + bdw[0:1] * ext_b[SUBLANES - 2:SUBLANES - 2 + tm])
    cat_ref[:, da:2 * da] = (proj(2) * y).astype(BF16)
    ext_b[0:SUBLANES] = m[tm - SUBLANES:tm]

    o_ref[0] = x + _dot(cat_ref[...], wout_ref[...])


def _conv_call(x, g, win, adw, adb, lng, lnb, bdw, wout, *, tm):
    B, S, D = x.shape
    ka, da = adw.shape
    assert ka - 1 <= A_HALO and tm % A_ROWS == 0 and tm % LN_ROWS == 0
    body = functools.partial(_conv_body, tm=tm, da=da, ka=ka)
    tile = pl.BlockSpec((1, tm, D), lambda b, s: (b, s, 0))
    consts = (g, win, adw, adb, lng, lnb, bdw, wout)
    return pl.pallas_call(
        body,
        out_shape=jax.ShapeDtypeStruct(x.shape, F32),
        grid=(B, S // tm),
        in_specs=[tile] + [_resident(c.shape) for c in consts],
        out_specs=tile,
        scratch_shapes=[
            pltpu.VMEM((tm, D), BF16),
            pltpu.VMEM((tm + A_HALO, da), F32),
            pltpu.VMEM((tm + SUBLANES, da), F32),
            pltpu.VMEM((tm, da), F32),
            pltpu.VMEM((tm, 2 * da), BF16),
        ],
        compiler_params=pltpu.CompilerParams(
            dimension_semantics=("arbitrary", "arbitrary"), vmem_limit_bytes=VMEM_LIMIT),
        name="conv_mixer",
    )(x, *consts)


QKV_COLS = 256


def _qkv_body(x_ref, g_ref, w_ref, qg_ref, kg_ref, q_ref, k_ref, v_ref, h_ref, *, tm, d):
    x = x_ref[0]
    h_ref[...] = _rmsnorm(x, g_ref[...]).astype(BF16)
    low = lax.broadcasted_iota(jnp.int32, (tm, LANES), 1) < HEAD_DIM

    def head_norm(t, gain):
        t2 = t * t
        s_lo = jnp.sum(jnp.where(low, t2, 0.0), axis=-1, keepdims=True)
        s_hi = jnp.sum(jnp.where(low, 0.0, t2), axis=-1, keepdims=True)
        ms = jnp.where(low, s_lo, s_hi) * (1.0 / HEAD_DIM)
        return t * lax.rsqrt(ms + EPS) * gain

    for part, (out_ref, gain_ref) in enumerate(((q_ref, qg_ref), (k_ref, kg_ref), (v_ref, None))):
        for c0 in range(0, d, QKV_COLS):
            y = _dot(h_ref[...], w_ref[:, part * d + c0:part * d + c0 + QKV_COLS])
            for l0 in range(0, QKV_COLS, LANES):
                t = y[:, l0:l0 + LANES]
                if gain_ref is not None:
                    t = head_norm(t, gain_ref[...])
                out_ref[0, :, c0 + l0:c0 + l0 + LANES] = t.astype(BF16)


def _qkv_call(x, g, w, qg, kg, *, tm):
    B, S, D = x.shape
    d = w.shape[1] // 3
    body = functools.partial(_qkv_body, tm=tm, d=d)
    tile = pl.BlockSpec((1, tm, D), lambda b, s: (b, s, 0))
    otile = pl.BlockSpec((1, tm, d), lambda b, s: (b, s, 0))
    out = jax.ShapeDtypeStruct((B, S, d), BF16)
    return pl.pallas_call(
        body,
        out_shape=(out, out, out),
        grid=(B, S // tm),
        in_specs=[tile, _resident(g.shape), _resident(w.shape), _resident(qg.shape),
                  _resident(kg.shape)],
        out_specs=(otile, otile, otile),
        scratch_shapes=[pltpu.VMEM((tm, D), BF16)],
        compiler_params=pltpu.CompilerParams(
            dimension_semantics=("arbitrary", "arbitrary"), vmem_limit_bytes=VMEM_LIMIT),
        name="qkv_proj",
    )(x, g, w, qg, kg)


LOG2E = 1.4426950408889634
RUN2_DONE = 160.0


def _softplus2(z2):
    neg_abs = lax.bitcast_convert_type(
        lax.bitcast_convert_type(z2, jnp.uint32) | jnp.uint32(0x80000000), F32)
    return jnp.maximum(z2, 0.0) + jnp.log2(1.0 + jnp.exp2(neg_abs))


def _attn_body(q_ref, k_ref, v_ref, tri_ref, o_ref, run_ref, acc_ref, *, tq, pairs):
    qi = pl.program_id(2)
    heads = range(2 * pairs)
    low = lax.broadcasted_iota(jnp.int32, (tq, LANES), 1) < HEAD_DIM

    def lane_tile(hd):
        return slice((hd // 2) * LANES, (hd // 2 + 1) * LANES)

    def q_head(hd):
        q = q_ref[0, :, lane_tile(hd)]
        zero = jnp.zeros_like(q)
        return jnp.where(low, q, zero) if hd % 2 == 0 else jnp.where(low, zero, q)

    q_heads = [q_head(hd) for hd in heads]

    def chain(hd, kb, valid, run):
        start = pl.multiple_of(kb * tq, tq)
        kblk = k_ref[0, pl.ds(start, tq), lane_tile(hd)]
        vblk = v_ref[0, pl.ds(start, tq), lane_tile(hd)]
        z2 = lax.dot_general(q_heads[hd], kblk, (((1,), (1,)), ((), ())),
                             preferred_element_type=F32) * LOG2E
        sp = _softplus2(z2)
        if valid is not None:
            sp = jnp.where(valid, sp, 0.0)
        suffix = _dot(sp.astype(BF16), tri_ref[...])
        arg = z2 - suffix
        if run is not None:
            arg = arg - jnp.concatenate([run] * (tq // LANES), axis=1)
        w = jnp.exp2(arg)
        if valid is not None:
            w = jnp.where(valid, w, 0.0)
        total = jnp.broadcast_to(suffix[:, 0:1], (tq, LANES))
        return _dot(w.astype(BF16), vblk), (total if run is None else run + total)

    def finish(accs):
        for p in range(pairs):
            o_ref[0, :, p * LANES:(p + 1) * LANES] = jnp.where(
                low, accs[2 * p], accs[2 * p + 1]).astype(BF16)

    row = lax.broadcasted_iota(jnp.int32, (tq, tq), 0)
    col = lax.broadcasted_iota(jnp.int32, (tq, tq), 1)
    below_diag = col < row

    @pl.when(qi == 0)
    def _():
        finish([chain(hd, 0, below_diag, None)[0] for hd in heads])

    @pl.when(qi > 0)
    def _():
        for hd in heads:
            pv_diag, run = chain(hd, qi, below_diag, None)
            pv_prev, run = chain(hd, qi - 1, None, run)
            acc_ref[hd] = pv_diag + pv_prev
            run_ref[hd] = run

        def more(state):
            i, min_run = state
            return jnp.logical_and(i < qi, min_run < RUN2_DONE)

        def earlier(state):
            i, _ = state
            for hd in heads:
                pv, run = chain(hd, qi - 1 - i, None, run_ref[hd])
                acc_ref[hd] += pv
                run_ref[hd] = run
            return i + 1, jnp.min(run_ref[...])

        lax.while_loop(more, earlier, (jnp.int32(1), jnp.min(run_ref[...])))
        finish([acc_ref[hd] for hd in heads])


ATTN_PAIRS = 2


def _attn_call(q, k, v, tri, *, tq):
    B, S, d = q.shape
    pairs = ATTN_PAIRS
    width = pairs * LANES
    body = functools.partial(_attn_body, tq=tq, pairs=pairs)
    qtile = pl.BlockSpec((1, tq, width), lambda b, h, i: (b, i, h))
    kvfull = pl.BlockSpec((1, S, width), lambda b, h, i: (b, 0, h))
    return pl.pallas_call(
        body,
        out_shape=jax.ShapeDtypeStruct((B, S, d), BF16),
        grid=(B, d // width, S // tq),
        in_specs=[qtile, kvfull, kvfull, _resident(tri.shape)],
        out_specs=qtile,
        scratch_shapes=[
            pltpu.VMEM((2 * pairs, tq, LANES), F32),
            pltpu.VMEM((2 * pairs, tq, LANES), F32),
        ],
        compiler_params=pltpu.CompilerParams(
            dimension_semantics=("arbitrary", "arbitrary", "arbitrary"),
            vmem_limit_bytes=VMEM_LIMIT),
        name="stickbreaking_attn",
    )(q, k, v, tri)


def _oproj_body(o_ref, w_ref, x_ref, out_ref):
    out_ref[0] = x_ref[0] + _dot(o_ref[0], w_ref[...])


def _oproj_call(o, w, x, *, tm):
    B, S, D = x.shape
    tile = pl.BlockSpec((1, tm, D), lambda b, s: (b, s, 0))
    otile = pl.BlockSpec((1, tm, o.shape[2]), lambda b, s: (b, s, 0))
    return pl.pallas_call(
        _oproj_body,
        out_shape=jax.ShapeDtypeStruct(x.shape, F32),
        grid=(B, S // tm),
        in_specs=[otile, _resident(w.shape), tile],
        out_specs=tile,
        compiler_params=pltpu.CompilerParams(
            dimension_semantics=("arbitrary", "arbitrary"), vmem_limit_bytes=VMEM_LIMIT),
        name="attn_out_proj",
    )(o, w, x)


FFN_CHUNK = 256


def _chunked_cols(a, cw):
    r, c = a.shape
    return a.reshape(r, c // cw, cw).transpose(1, 0, 2)


def kernel(x, mix_norm_g, ffn_norm_g, conv_w_in, conv_a_dw_w, conv_a_dw_b, conv_a_ln_g,
           conv_a_ln_b, conv_b_dw_w, conv_w_out, attn_w_qkv, attn_q_g, attn_k_g, attn_w_o,
           ffn_w_up, ffn_dw_w, ffn_dw_b, ffn_w_down):
    B, S, D = x.shape
    depth = mix_norm_g.shape[0]
    d_ff = ffn_w_down.shape[1]
    tm = min(512, S)
    tm_conv = min(256, S)
    tq = min(256, S)
    cw = FFN_CHUNK
    scale = HEAD_DIM ** -0.5

    tri = (lax.broadcasted_iota(jnp.int32, (tq, tq), 0)
           >= lax.broadcasted_iota(jnp.int32, (tq, tq), 1)).astype(BF16)

    for layer in range(depth):
        i = layer // 2
        g = mix_norm_g[layer][None, :]
        if layer % 2 == 0:
            x = _conv_call(
                x, g, conv_w_in[i].astype(BF16), conv_a_dw_w[i], conv_a_dw_b[i][None, :],
                conv_a_ln_g[i][None, :], conv_a_ln_b[i][None, :], conv_b_dw_w[i],
                conv_w_out[i].astype(BF16), tm=tm_conv)
        else:
            qg = jnp.tile(attn_q_g[i] * scale, LANES // HEAD_DIM)[None, :]
            kg = jnp.tile(attn_k_g[i], LANES // HEAD_DIM)[None, :]
            q, k, v = _qkv_call(x, g, attn_w_qkv[i].astype(BF16), qg, kg, tm=tm)
            o = _attn_call(q, k, v, tri, tq=tq)
            x = _oproj_call(o, attn_w_o[i].astype(BF16), x, tm=tm)

        w_up = ffn_w_up[layer].astype(BF16)
        dw = ffn_dw_w[layer]
        db = ffn_dw_b[layer][None, :]
        x = _ffn_call(
            x, ffn_norm_g[layer][None, :],
            _chunked_cols(w_up[:, :d_ff], cw), _chunked_cols(w_up[:, d_ff:], cw),
            _chunked_cols(dw[:, :d_ff], cw), _chunked_cols(dw[:, d_ff:], cw),
            _chunked_cols(db[:, :d_ff], cw), _chunked_cols(db[:, d_ff:], cw),
            ffn_w_down[layer].astype(BF16).reshape(d_ff // cw, cw, D), tm=tm)
    return x
```

```python
import functools

import jax
import jax.numpy as jnp
from jax import lax
from jax.experimental import pallas as pl
from jax.experimental.pallas import tpu as pltpu

EPS = 1e-6
HEAD_DIM = 64
LANES = 128
SUBLANES = 8
VMEM_LIMIT = 56 * 1024 * 1024

BF16 = jnp.bfloat16
F32 = jnp.float32


def _resident(shape):
    zeros = (0,) * len(shape)
    return pl.BlockSpec(shape, lambda *_: zeros, pipeline_mode=pl.Buffered(1))


def _rmsnorm(x, g):
    return x * lax.rsqrt(jnp.mean(x * x, axis=-1, keepdims=True) + EPS) * g


def _dot(a, b):
    return jnp.dot(a, b, preferred_element_type=F32)


HDR = 2 * SUBLANES
GROUP = SUBLANES * SUBLANES
GROUP_ROWS = HDR + GROUP


def _ffn_body(x_ref, g_ref, wg_ref, wv_ref, dwg_ref, dwv_ref, bg_ref, bv_ref, wd_ref,
              o_ref, st_ref, h_ref, ext_g0, ext_v0, ext_g1, ext_v1, carry_g, carry_v, act_ref,
              *, tm, nc):
    assert nc % 2 == 1 and tm % GROUP == 0
    ext_g, ext_v = (ext_g0, ext_g1), (ext_v0, ext_v1)
    groups = tm // GROUP
    lane_tiles = st_ref.shape[0]

    @pl.when(pl.program_id(1) == 0)
    def _():
        carry_g[...] = jnp.zeros_like(carry_g)
        carry_v[...] = jnp.zeros_like(carry_v)

    def transposed_slab(l, g, j):
        return st_ref[l, pl.ds(g * GROUP + j, SUBLANES, stride=SUBLANES), :]

    hn = _rmsnorm(x_ref[0], g_ref[...])
    for l in range(lane_tiles):
        st_ref[l] = hn[:, l * LANES:(l + 1) * LANES]
    for g in range(groups):
        for j in range(0, SUBLANES, 2):
            r0 = g * GROUP + j * SUBLANES
            for l in range(lane_tiles):
                pair = [transposed_slab(l, g, j + i) for i in range(2)]
                h_ref[r0:r0 + 2 * SUBLANES, l * LANES:(l + 1) * LANES] = (
                    jnp.concatenate(pair, axis=0).astype(BF16))

    def up(c, slot):
        for w_ref, ext in ((wg_ref, ext_g[slot]), (wv_ref, ext_v[slot])):
            u = _dot(h_ref[...], w_ref[c])
            for g in range(groups):
                ext[g * GROUP_ROWS + HDR:(g + 1) * GROUP_ROWS] = u[g * GROUP:(g + 1) * GROUP]

    last_sublane = lax.broadcasted_iota(jnp.int32, (SUBLANES, 1), 0) == SUBLANES - 1

    def conv3(c, dw_ref, b_ref, ext, carry):
        dw = dw_ref[c]
        out = []
        for g in range(groups):
            base = g * GROUP_ROWS
            for i in range(2):
                tail = ext[base + GROUP + i * SUBLANES:base + GROUP + (i + 1) * SUBLANES]
                if g == 0:
                    prev = carry[c, i * SUBLANES:(i + 1) * SUBLANES]
                else:
                    prev = ext[base - HDR + i * SUBLANES:base - HDR + (i + 1) * SUBLANES]
                ext[base + i * SUBLANES:base + (i + 1) * SUBLANES] = pltpu.roll(
                    jnp.where(last_sublane, prev, tail), 1, axis=0)
            out.append(dw[2:3] * ext[base + HDR:base + HDR + GROUP]
                       + dw[1:2] * ext[base + SUBLANES:base + SUBLANES + GROUP]
                       + dw[0:1] * ext[base:base + GROUP] + b_ref[c])
        carry[c] = ext[groups * GROUP_ROWS - HDR:groups * GROUP_ROWS]
        return jnp.concatenate(out, axis=0)

    def activate(c, slot):
        gate = conv3(c, dwg_ref, bg_ref, ext_g[slot], carry_g)
        val = conv3(c, dwv_ref, bv_ref, ext_v[slot], carry_v)
        act_ref[c] = (gate * jax.nn.sigmoid(gate) * val).astype(BF16)

    def chunk_pair(j, carry):
        c = 2 * j
        activate(c, 0)
        up(c + 1, 1)
        activate(c + 1, 1)
        up(c + 2, 0)
        return carry

    up(0, 0)
    lax.fori_loop(0, (nc - 1) // 2, chunk_pair, 0)
    activate(nc - 1, 0)

    acc = _dot(act_ref[0], wd_ref[0])
    for c in range(1, nc):
        acc = acc + _dot(act_ref[c], wd_ref[c])
    for l in range(lane_tiles):
        st_ref[l] = acc[:, l * LANES:(l + 1) * LANES]
    for g in range(groups):
        for s in range(SUBLANES):
            rows = slice(g * GROUP + s * SUBLANES, g * GROUP + (s + 1) * SUBLANES)
            for l in range(lane_tiles):
                lanes = slice(l * LANES, (l + 1) * LANES)
                o_ref[0, rows, lanes] = x_ref[0, rows, lanes] + transposed_slab(l, g, s)


def _ffn_call(x, g, wg, wv, dwg, dwv, bg, bv, wd, *, tm):
    B, S, D = x.shape
    nc, _, cw = wg.shape
    body = functools.partial(_ffn_body, tm=tm, nc=nc)
    tile = pl.BlockSpec((1, tm, D), lambda b, s: (b, s, 0))
    return pl.pallas_call(
        body,
        out_shape=jax.ShapeDtypeStruct(x.shape, F32),
        grid=(B, S // tm),
        in_specs=[tile, _resident(g.shape), _resident(wg.shape), _resident(wv.shape),
                  _resident(dwg.shape), _resident(dwv.shape), _resident(bg.shape),
                  _resident(bv.shape), _resident(wd.shape)],
        out_specs=tile,
        scratch_shapes=[
            pltpu.VMEM((D // LANES, tm, LANES), F32),
            pltpu.VMEM((tm, D), BF16),
            pltpu.VMEM((tm // GROUP * GROUP_ROWS, cw), F32),
            pltpu.VMEM((tm // GROUP * GROUP_ROWS, cw), F32),
            pltpu.VMEM((tm // GROUP * GROUP_ROWS, cw), F32),
            pltpu.VMEM((tm // GROUP * GROUP_ROWS, cw), F32),
            pltpu.VMEM((nc, HDR, cw), F32),
            pltpu.VMEM((nc, HDR, cw), F32),
            pltpu.VMEM((nc, tm, cw), BF16),
        ],
        compiler_params=pltpu.CompilerParams(
            dimension_semantics=("arbitrary", "arbitrary"), vmem_limit_bytes=VMEM_LIMIT),
        name="conv_ffn",
    )(x, g, wg, wv, dwg, dwv, bg, bv, wd)


A_HALO = 32
A_ROWS = 64
LN_ROWS = 32


def _conv_body(x_ref, g_ref, win_ref, adw_ref, adb_ref, lng_ref, lnb_ref, bdw_ref, wout_ref,
               o_ref, h_ref, ext_a, ext_b, conv_ref, cat_ref, *, tm, da, ka):
    @pl.when(pl.program_id(1) == 0)
    def _():
        ext_a[0:A_HALO] = jnp.zeros((A_HALO, da), F32)
        ext_b[0:SUBLANES] = jnp.zeros((SUBLANES, da), F32)

    x = x_ref[0]
    h_ref[...] = _rmsnorm(x, g_ref[...]).astype(BF16)

    def proj(i):
        return _dot(h_ref[...], win_ref[:, i * da:(i + 1) * da])

    ext_a[A_HALO:A_HALO + tm] = proj(0) * jax.nn.sigmoid(proj(1))
    base = A_HALO - (ka - 1)
    for r0 in range(0, tm, A_ROWS):
        for l0 in range(0, da, LANES):
            lanes = slice(l0, l0 + LANES)
            acc = jnp.broadcast_to(adb_ref[:, lanes], (A_ROWS, LANES))
            for phase in range(SUBLANES):
                rows = A_ROWS if phase == 0 else A_ROWS + SUBLANES
                part = None
                for k in range(ka):
                    if (base + k) % SUBLANES != phase:
                        continue
                    start = r0 + (base + k) - phase
                    term = adw_ref[k:k + 1, lanes] * ext_a[start:start + rows, lanes]
                    part = term if part is None else part + term
                acc = acc + part[phase:phase + A_ROWS]
            conv_ref[r0:r0 + A_ROWS, lanes] = acc
    for r0 in range(0, tm, LN_ROWS):
        acc = conv_ref[r0:r0 + LN_ROWS]
        mu = jnp.mean(acc, axis=-1, keepdims=True)
        xc = acc - mu
        y = xc * lax.rsqrt(jnp.mean(xc * xc, axis=-1, keepdims=True) + EPS)
        y = y * lng_ref[...] + lnb_ref[...]
        cat_ref[r0:r0 + LN_ROWS, 0:da] = (y * jax.nn.sigmoid(y)).astype(BF16)
    ext_a[0:A_HALO] = ext_a[tm:tm + A_HALO]

    m = proj(3) * proj(4)
    ext_b[SUBLANES:SUBLANES + tm] = m
    bdw = bdw_ref[...]
    y = (bdw[2:3] * m + bdw[1:2] * ext_b[SUBLANES - 1:SUBLANES - 1 + tm]
         + bdw[0:1] * ext_b[SUBLANES - 2:SUBLANES - 2 + tm])
    cat_ref[:, da:2 * da] = (proj(2) * y).astype(BF16)
    ext_b[0:SUBLANES] = m[tm - SUBLANES:tm]

    o_ref[0] = x + _dot(cat_ref[...], wout_ref[...])


def _conv_call(x, g, win, adw, adb, lng, lnb, bdw, wout, *, tm):
    B, S, D = x.shape
    ka, da = adw.shape
    assert ka - 1 <= A_HALO and tm % A_ROWS == 0 and tm % LN_ROWS == 0
    body = functools.partial(_conv_body, tm=tm, da=da, ka=ka)
    tile = pl.BlockSpec((1, tm, D), lambda b, s: (b, s, 0))
    consts = (g, win, adw, adb, lng, lnb, bdw, wout)
    return pl.pallas_call(
        body,
        out_shape=jax.ShapeDtypeStruct(x.shape, F32),
        grid=(B, S // tm),
        in_specs=[tile] + [_resident(c.shape) for c in consts],
        out_specs=tile,
        scratch_shapes=[
            pltpu.VMEM((tm, D), BF16),
            pltpu.VMEM((tm + A_HALO, da), F32),
            pltpu.VMEM((tm + SUBLANES, da), F32),
            pltpu.VMEM((tm, da), F32),
            pltpu.VMEM((tm, 2 * da), BF16),
        ],
        compiler_params=pltpu.CompilerParams(
            dimension_semantics=("arbitrary", "arbitrary"), vmem_limit_bytes=VMEM_LIMIT),
        name="conv_mixer",
    )(x, *consts)


QKV_COLS = 256


def _qkv_body(x_ref, g_ref, w_ref, qg_ref, kg_ref, q_ref, k_ref, v_ref, h_ref, *, tm, d):
    x = x_ref[0]
    h_ref[...] = _rmsnorm(x, g_ref[...]).astype(BF16)
    low = lax.broadcasted_iota(jnp.int32, (tm, LANES), 1) < HEAD_DIM

    def head_norm(t, gain):
        t2 = t * t
        s_lo = jnp.sum(jnp.where(low, t2, 0.0), axis=-1, keepdims=True)
        s_hi = jnp.sum(jnp.where(low, 0.0, t2), axis=-1, keepdims=True)
        ms = jnp.where(low, s_lo, s_hi) * (1.0 / HEAD_DIM)
        return t * lax.rsqrt(ms + EPS) * gain

    for part, (out_ref, gain_ref) in enumerate(((q_ref, qg_ref), (k_ref, kg_ref), (v_ref, None))):
        for c0 in range(0, d, QKV_COLS):
            y = _dot(h_ref[...], w_ref[:, part * d + c0:part * d + c0 + QKV_COLS])
            for l0 in range(0, QKV_COLS, LANES):
                t = y[:, l0:l0 + LANES]
                if gain_ref is not None:
                    t = head_norm(t, gain_ref[...])
                out_ref[0, :, c0 + l0:c0 + l0 + LANES] = t.astype(BF16)


def _qkv_call(x, g, w, qg, kg, *, tm):
    B, S, D = x.shape
    d = w.shape[1] // 3
    body = functools.partial(_qkv_body, tm=tm, d=d)
    tile = pl.BlockSpec((1, tm, D), lambda b, s: (b, s, 0))
    otile = pl.BlockSpec((1, tm, d), lambda b, s: (b, s, 0))
    out = jax.ShapeDtypeStruct((B, S, d), BF16)
    return pl.pallas_call(
        body,
        out_shape=(out, out, out),
        grid=(B, S // tm),
        in_specs=[tile, _resident(g.shape), _resident(w.shape), _resident(qg.shape),
                  _resident(kg.shape)],
        out_specs=(otile, otile, otile),
        scratch_shapes=[pltpu.VMEM((tm, D), BF16)],
        compiler_params=pltpu.CompilerParams(
            dimension_semantics=("arbitrary", "arbitrary"), vmem_limit_bytes=VMEM_LIMIT),
        name="qkv_proj",
    )(x, g, w, qg, kg)


LOG2E = 1.4426950408889634
RUN2_DONE = 160.0


def _softplus2(z2):
    neg_abs = lax.bitcast_convert_type(
        lax.bitcast_convert_type(z2, jnp.uint32) | jnp.uint32(0x80000000), F32)
    return jnp.maximum(z2, 0.0) + jnp.log2(1.0 + jnp.exp2(neg_abs))


def _attn_body(q_ref, k_ref, v_ref, tri_ref, o_ref, run_ref, acc_ref, *, tq, pairs):
    qi = pl.program_id(2)
    heads = range(2 * pairs)
    low = lax.broadcasted_iota(jnp.int32, (tq, LANES), 1) < HEAD_DIM

    def lane_tile(hd):
        return slice((hd // 2) * LANES, (hd // 2 + 1) * LANES)

    def q_head(hd):
        q = q_ref[0, :, lane_tile(hd)]
        zero = jnp.zeros_like(q)
        return jnp.where(low, q, zero) if hd % 2 == 0 else jnp.where(low, zero, q)

    q_heads = [q_head(hd) for hd in heads]

    def chain(hd, kb, valid, run):
        start = pl.multiple_of(kb * tq, tq)
        kblk = k_ref[0, pl.ds(start, tq), lane_tile(hd)]
        vblk = v_ref[0, pl.ds(start, tq), lane_tile(hd)]
        z2 = lax.dot_general(q_heads[hd], kblk, (((1,), (1,)), ((), ())),
                             preferred_element_type=F32) * LOG2E
        sp = _softplus2(z2)
        if valid is not None:
            sp = jnp.where(valid, sp, 0.0)
        suffix = _dot(sp.astype(BF16), tri_ref[...])
        arg = z2 - suffix
        if run is not None:
            arg = arg - jnp.concatenate([run] * (tq // LANES), axis=1)
        w = jnp.exp2(arg)
        if valid is not None:
            w = jnp.where(valid, w, 0.0)
        total = jnp.broadcast_to(suffix[:, 0:1], (tq, LANES))
        return _dot(w.astype(BF16), vblk), (total if run is None else run + total)

    def finish(accs):
        for p in range(pairs):
            o_ref[0, :, p * LANES:(p + 1) * LANES] = jnp.where(
                low, accs[2 * p], accs[2 * p + 1]).astype(BF16)

    row = lax.broadcasted_iota(jnp.int32, (tq, tq), 0)
    col = lax.broadcasted_iota(jnp.int32, (tq, tq), 1)
    below_diag = col < row

    @pl.when(qi == 0)
    def _():
        finish([chain(hd, 0, below_diag, None)[0] for hd in heads])

    @pl.when(qi > 0)
    def _():
        for hd in heads:
            pv_diag, run = chain(hd, qi, below_diag, None)
            pv_prev, run = chain(hd, qi - 1, None, run)
            acc_ref[hd] = pv_diag + pv_prev
            run_ref[hd] = run

        def more(state):
            i, min_run = state
            return jnp.logical_and(i < qi, min_run < RUN2_DONE)

        def earlier(state):
            i, _ = state
            for hd in heads:
                pv, run = chain(hd, qi - 1 - i, None, run_ref[hd])
                acc_ref[hd] += pv
                run_ref[hd] = run
            return i + 1, jnp.min(run_ref[...])

        lax.while_loop(more, earlier, (jnp.int32(1), jnp.min(run_ref[...])))
        finish([acc_ref[hd] for hd in heads])


ATTN_PAIRS = 2


def _attn_call(q, k, v, tri, *, tq):
    B, S, d = q.shape
    pairs = ATTN_PAIRS
    width = pairs * LANES
    body = functools.partial(_attn_body, tq=tq, pairs=pairs)
    qtile = pl.BlockSpec((1, tq, width), lambda b, h, i: (b, i, h))
    kvfull = pl.BlockSpec((1, S, width), lambda b, h, i: (b, 0, h))
    return pl.pallas_call(
        body,
        out_shape=jax.ShapeDtypeStruct((B, S, d), BF16),
        grid=(B, d // width, S // tq),
        in_specs=[qtile, kvfull, kvfull, _resident(tri.shape)],
        out_specs=qtile,
        scratch_shapes=[
            pltpu.VMEM((2 * pairs, tq, LANES), F32),
            pltpu.VMEM((2 * pairs, tq, LANES), F32),
        ],
        compiler_params=pltpu.CompilerParams(
            dimension_semantics=("arbitrary", "arbitrary", "arbitrary"),
            vmem_limit_bytes=VMEM_LIMIT),
        name="stickbreaking_attn",
    )(q, k, v, tri)


def _oproj_body(o_ref, w_ref, x_ref, out_ref):
    out_ref[0] = x_ref[0] + _dot(o_ref[0], w_ref[...])


def _oproj_call(o, w, x, *, tm):
    B, S, D = x.shape
    tile = pl.BlockSpec((1, tm, D), lambda b, s: (b, s, 0))
    otile = pl.BlockSpec((1, tm, o.shape[2]), lambda b, s: (b, s, 0))
    return pl.pallas_call(
        _oproj_body,
        out_shape=jax.ShapeDtypeStruct(x.shape, F32),
        grid=(B, S // tm),
        in_specs=[otile, _resident(w.shape), tile],
        out_specs=tile,
        compiler_params=pltpu.CompilerParams(
            dimension_semantics=("arbitrary", "arbitrary"), vmem_limit_bytes=VMEM_LIMIT),
        name="attn_out_proj",
    )(o, w, x)


FFN_CHUNK = 256


def _chunked_cols(a, cw):
    r, c = a.shape
    return a.reshape(r, c // cw, cw).transpose(1, 0, 2)


def kernel(x, mix_norm_g, ffn_norm_g, conv_w_in, conv_a_dw_w, conv_a_dw_b, conv_a_ln_g,
           conv_a_ln_b, conv_b_dw_w, conv_w_out, attn_w_qkv, attn_q_g, attn_k_g, attn_w_o,
           ffn_w_up, ffn_dw_w, ffn_dw_b, ffn_w_down):
    B, S, D = x.shape
    depth = mix_norm_g.shape[0]
    d_ff = ffn_w_down.shape[1]
    tm = min(512, S)
    tm_conv = min(256, S)
    tq = min(256, S)
    cw = FFN_CHUNK
    scale = HEAD_DIM ** -0.5

    tri = (lax.broadcasted_iota(jnp.int32, (tq, tq), 0)
           >= lax.broadcasted_iota(jnp.int32, (tq, tq), 1)).astype(BF16)

    for layer in range(depth):
        i = layer // 2
        g = mix_norm_g[layer][None, :]
        if layer % 2 == 0:
            x = _conv_call(
                x, g, conv_w_in[i].astype(BF16), conv_a_dw_w[i], conv_a_dw_b[i][None, :],
                conv_a_ln_g[i][None, :], conv_a_ln_b[i][None, :], conv_b_dw_w[i],
                conv_w_out[i].astype(BF16), tm=tm_conv)
        else:
            qg = jnp.tile(attn_q_g[i] * scale, LANES // HEAD_DIM)[None, :]
            kg = jnp.tile(attn_k_g[i], LANES // HEAD_DIM)[None, :]
            q, k, v = _qkv_call(x, g, attn_w_qkv[i].astype(BF16), qg, kg, tm=tm)
            o = _attn_call(q, k, v, tri, tq=tq)
            x = _oproj_call(o, attn_w_o[i].astype(BF16), x, tm=tm)

        w_up = ffn_w_up[layer].astype(BF16)
        dw = ffn_dw_w[layer]
        db = ffn_dw_b[layer][None, :]
        x = _ffn_call(
            x, ffn_norm_g[layer][None, :],
            _chunked_cols(w_up[:, :d_ff], cw), _chunked_cols(w_up[:, d_ff:], cw),
            _chunked_cols(dw[:, :d_ff], cw), _chunked_cols(dw[:, d_ff:], cw),
            _chunked_cols(db[:, :d_ff], cw), _chunked_cols(db[:, d_ff:], cw),
            ffn_w_down[layer].astype(BF16).reshape(d_ff // cw, cw, D), tm=tm)
    return x
```

```python
import functools
from typing import NamedTuple

import jax
import jax.numpy as jnp
from jax import lax
from jax.experimental import pallas as pl
from jax.experimental.pallas import tpu as pltpu

EPS = 1e-6
HEAD_DIM = 64
LANES = 128
SUBLANES = 8
VMEM_LIMIT = 56 * 1024 * 1024

BF16 = jnp.bfloat16
F32 = jnp.float32


class _Sel(NamedTuple):
    arr: jax.Array
    lead: tuple = ()

    @property
    def shape(self):
        return self.arr.shape[len(self.lead):]


def _resident(op):
    n = len(op.lead)
    index = tuple(op.lead) + (0,) * (op.arr.ndim - n)
    return pl.BlockSpec((None,) * n + op.shape, lambda *_: index, pipeline_mode=pl.Buffered(1))


def _rmsnorm(x, g):
    return x * lax.rsqrt(jnp.mean(x * x, axis=-1, keepdims=True) + EPS) * g


def _dot(a, b):
    return jnp.dot(a, b, preferred_element_type=F32)


HDR = 2 * SUBLANES
GROUP = SUBLANES * SUBLANES
GROUP_ROWS = HDR + GROUP


def _ffn_body(x_ref, g_ref, wg_ref, wv_ref, dwg_ref, dwv_ref, bg_ref, bv_ref, wd_ref,
              o_ref, st_ref, h_ref, ext_g0, ext_v0, ext_g1, ext_v1, carry_g, carry_v, act_ref,
              *, tm, nc):
    assert nc % 2 == 1 and tm % GROUP == 0
    ext_g, ext_v = (ext_g0, ext_g1), (ext_v0, ext_v1)
    groups = tm // GROUP
    lane_tiles = st_ref.shape[0]

    @pl.when(pl.program_id(1) == 0)
    def _():
        carry_g[...] = jnp.zeros_like(carry_g)
        carry_v[...] = jnp.zeros_like(carry_v)

    def transposed_slab(l, g, j):
        return st_ref[l, pl.ds(g * GROUP + j, SUBLANES, stride=SUBLANES), :]

    hn = _rmsnorm(x_ref[0], g_ref[...])
    for l in range(lane_tiles):
        st_ref[l] = hn[:, l * LANES:(l + 1) * LANES]
    for g in range(groups):
        for j in range(0, SUBLANES, 2):
            r0 = g * GROUP + j * SUBLANES
            for l in range(lane_tiles):
                pair = [transposed_slab(l, g, j + i) for i in range(2)]
                h_ref[r0:r0 + 2 * SUBLANES, l * LANES:(l + 1) * LANES] = (
                    jnp.concatenate(pair, axis=0).astype(BF16))

    def up(c, slot):
        for w_ref, ext in ((wg_ref, ext_g[slot]), (wv_ref, ext_v[slot])):
            u = _dot(h_ref[...], w_ref[c])
            for g in range(groups):
                ext[g * GROUP_ROWS + HDR:(g + 1) * GROUP_ROWS] = u[g * GROUP:(g + 1) * GROUP]

    last_sublane = lax.broadcasted_iota(jnp.int32, (SUBLANES, 1), 0) == SUBLANES - 1

    def conv3(c, dw_ref, b_ref, ext, carry):
        dw = dw_ref[c]
        out = []
        for g in range(groups):
            base = g * GROUP_ROWS
            for i in range(2):
                tail = ext[base + GROUP + i * SUBLANES:base + GROUP + (i + 1) * SUBLANES]
                if g == 0:
                    prev = carry[c, i * SUBLANES:(i + 1) * SUBLANES]
                else:
                    prev = ext[base - HDR + i * SUBLANES:base - HDR + (i + 1) * SUBLANES]
                ext[base + i * SUBLANES:base + (i + 1) * SUBLANES] = pltpu.roll(
                    jnp.where(last_sublane, prev, tail), 1, axis=0)
            out.append(dw[2:3] * ext[base + HDR:base + HDR + GROUP]
                       + dw[1:2] * ext[base + SUBLANES:base + SUBLANES + GROUP]
                       + dw[0:1] * ext[base:base + GROUP] + b_ref[c])
        carry[c] = ext[groups * GROUP_ROWS - HDR:groups * GROUP_ROWS]
        return jnp.concatenate(out, axis=0)

    def activate(c, slot):
        gate = conv3(c, dwg_ref, bg_ref, ext_g[slot], carry_g)
        val = conv3(c, dwv_ref, bv_ref, ext_v[slot], carry_v)
        act_ref[c] = (gate * jax.nn.sigmoid(gate) * val).astype(BF16)

    def chunk_pair(j, carry):
        c = 2 * j
        activate(c, 0)
        up(c + 1, 1)
        activate(c + 1, 1)
        up(c + 2, 0)
        return carry

    up(0, 0)
    lax.fori_loop(0, (nc - 1) // 2, chunk_pair, 0)
    activate(nc - 1, 0)

    acc = _dot(act_ref[0], wd_ref[0])
    for c in range(1, nc):
        acc = acc + _dot(act_ref[c], wd_ref[c])
    for l in range(lane_tiles):
        st_ref[l] = acc[:, l * LANES:(l + 1) * LANES]
    for g in range(groups):
        for s in range(SUBLANES):
            rows = slice(g * GROUP + s * SUBLANES, g * GROUP + (s + 1) * SUBLANES)
            for l in range(lane_tiles):
                lanes = slice(l * LANES, (l + 1) * LANES)
                o_ref[0, rows, lanes] = x_ref[0, rows, lanes] + transposed_slab(l, g, s)


def _ffn_call(x, g, wg, wv, dwg, dwv, bg, bv, wd, *, tm):
    B, S, D = x.shape
    nc, _, cw = wg.shape
    body = functools.partial(_ffn_body, tm=tm, nc=nc)
    tile = pl.BlockSpec((1, tm, D), lambda b, s: (b, s, 0))
    consts = (g, wg, wv, dwg, dwv, bg, bv, wd)
    return pl.pallas_call(
        body,
        out_shape=jax.ShapeDtypeStruct(x.shape, F32),
        grid=(B, S // tm),
        in_specs=[tile] + [_resident(c) for c in consts],
        out_specs=tile,
        scratch_shapes=[
            pltpu.VMEM((D // LANES, tm, LANES), F32),
            pltpu.VMEM((tm, D), BF16),
            pltpu.VMEM((tm // GROUP * GROUP_ROWS, cw), F32),
            pltpu.VMEM((tm // GROUP * GROUP_ROWS, cw), F32),
            pltpu.VMEM((tm // GROUP * GROUP_ROWS, cw), F32),
            pltpu.VMEM((tm // GROUP * GROUP_ROWS, cw), F32),
            pltpu.VMEM((nc, HDR, cw), F32),
            pltpu.VMEM((nc, HDR, cw), F32),
            pltpu.VMEM((nc, tm, cw), BF16),
        ],
        compiler_params=pltpu.CompilerParams(
            dimension_semantics=("arbitrary", "arbitrary"), vmem_limit_bytes=VMEM_LIMIT),
        name="conv_ffn",
    )(x, *[c.arr for c in consts])


A_HALO = 32
A_ROWS = 64
LN_ROWS = 32


def _conv_body(x_ref, g_ref, win_ref, adw_ref, adb_ref, lng_ref, lnb_ref, bdw_ref, wout_ref,
               o_ref, h_ref, ext_a, ext_b, conv_ref, cat_ref, *, tm, da, ka):
    @pl.when(pl.program_id(1) == 0)
    def _():
        ext_a[0:A_HALO] = jnp.zeros((A_HALO, da), F32)
        ext_b[0:SUBLANES] = jnp.zeros((SUBLANES, da), F32)

    x = x_ref[0]
    h_ref[...] = _rmsnorm(x, g_ref[...]).astype(BF16)

    def proj(i):
        return _dot(h_ref[...], win_ref[:, i * da:(i + 1) * da])

    ext_a[A_HALO:A_HALO + tm] = proj(0) * jax.nn.sigmoid(proj(1))
    base = A_HALO - (ka - 1)
    for r0 in range(0, tm, A_ROWS):
        for l0 in range(0, da, LANES):
            lanes = slice(l0, l0 + LANES)
            acc = jnp.broadcast_to(adb_ref[:, lanes], (A_ROWS, LANES))
            for phase in range(SUBLANES):
                rows = A_ROWS if phase == 0 else A_ROWS + SUBLANES
                part = None
                for k in range(ka):
                    if (base + k) % SUBLANES != phase:
                        continue
                    start = r0 + (base + k) - phase
                    term = adw_ref[k:k + 1, lanes] * ext_a[start:start + rows, lanes]
                    part = term if part is None else part + term
                acc = acc + part[phase:phase + A_ROWS]
            conv_ref[r0:r0 + A_ROWS, lanes] = acc
    for r0 in range(0, tm, LN_ROWS):
        acc = conv_ref[r0:r0 + LN_ROWS]
        mu = jnp.mean(acc, axis=-1, keepdims=True)
        xc = acc - mu
        y = xc * lax.rsqrt(jnp.mean(xc * xc, axis=-1, keepdims=True) + EPS)
        y = y * lng_ref[...] + lnb_ref[...]
        cat_ref[r0:r0 + LN_ROWS, 0:da] = (y * jax.nn.sigmoid(y)).astype(BF16)
    ext_a[0:A_HALO] = ext_a[tm:tm + A_HALO]

    m = proj(3) * proj(4)
    ext_b[SUBLANES:SUBLANES + tm] = m
    bdw = bdw_ref[...]
    y = (bdw[2:3] * m + bdw[1:2] * ext_b[SUBLANES - 1:SUBLANES - 1 + tm]
         + bdw[0:1] * ext_b[SUBLANES - 2:SUBLANES - 2 + tm])
    cat_ref[:, da:2 * da] = (proj(2) * y).astype(BF16)
    ext_b[0:SUBLANES] = m[tm - SUBLANES:tm]

    o_ref[0] = x + _dot(cat_ref[...], wout_ref[...])


def _conv_call(x, g, win, adw, adb, lng, lnb, bdw, wout, *, tm):
    B, S, D = x.shape
    ka, da = adw.shape
    assert ka - 1 <= A_HALO and tm % A_ROWS == 0 and tm % LN_ROWS == 0
    body = functools.partial(_conv_body, tm=tm, da=da, ka=ka)
    tile = pl.BlockSpec((1, tm, D), lambda b, s: (b, s, 0))
    consts = (g, win, adw, adb, lng, lnb, bdw, wout)
    return pl.pallas_call(
        body,
        out_shape=jax.ShapeDtypeStruct(x.shape, F32),
        grid=(B, S // tm),
        in_specs=[tile] + [_resident(c) for c in consts],
        out_specs=tile,
        scratch_shapes=[
            pltpu.VMEM((tm, D), BF16),
            pltpu.VMEM((tm + A_HALO, da), F32),
            pltpu.VMEM((tm + SUBLANES, da), F32),
            pltpu.VMEM((tm, da), F32),
            pltpu.VMEM((tm, 2 * da), BF16),
        ],
        compiler_params=pltpu.CompilerParams(
            dimension_semantics=("arbitrary", "arbitrary"), vmem_limit_bytes=VMEM_LIMIT),
        name="conv_mixer",
    )(x, *[c.arr for c in consts])


QKV_COLS = 256


def _qkv_body(x_ref, g_ref, w_ref, qg_ref, kg_ref, q_ref, k_ref, v_ref, h_ref, *, tm, d):
    x = x_ref[0]
    h_ref[...] = _rmsnorm(x, g_ref[...]).astype(BF16)
    low = lax.broadcasted_iota(jnp.int32, (tm, LANES), 1) < HEAD_DIM

    def head_norm(t, gain):
        t2 = t * t
        s_lo = jnp.sum(jnp.where(low, t2, 0.0), axis=-1, keepdims=True)
        s_hi = jnp.sum(jnp.where(low, 0.0, t2), axis=-1, keepdims=True)
        ms = jnp.where(low, s_lo, s_hi) * (1.0 / HEAD_DIM)
        return t * lax.rsqrt(ms + EPS) * gain

    for part, (out_ref, gain_ref) in enumerate(((q_ref, qg_ref), (k_ref, kg_ref), (v_ref, None))):
        for c0 in range(0, d, QKV_COLS):
            y = _dot(h_ref[...], w_ref[:, part * d + c0:part * d + c0 + QKV_COLS])
            for l0 in range(0, QKV_COLS, LANES):
                t = y[:, l0:l0 + LANES]
                if gain_ref is not None:
                    t = head_norm(t, gain_ref[...])
                out_ref[0, :, c0 + l0:c0 + l0 + LANES] = t.astype(BF16)


def _qkv_call(x, g, w, qg, kg, *, tm):
    B, S, D = x.shape
    d = w.shape[1] // 3
    body = functools.partial(_qkv_body, tm=tm, d=d)
    tile = pl.BlockSpec((1, tm, D), lambda b, s: (b, s, 0))
    otile = pl.BlockSpec((1, tm, d), lambda b, s: (b, s, 0))
    out = jax.ShapeDtypeStruct((B, S, d), BF16)
    return pl.pallas_call(
        body,
        out_shape=(out, out, out),
        grid=(B, S // tm),
        in_specs=[tile, _resident(g), _resident(w), _resident(qg), _resident(kg)],
        out_specs=(otile, otile, otile),
        scratch_shapes=[pltpu.VMEM((tm, D), BF16)],
        compiler_params=pltpu.CompilerParams(
            dimension_semantics=("arbitrary", "arbitrary"), vmem_limit_bytes=VMEM_LIMIT),
        name="qkv_proj",
    )(x, g.arr, w.arr, qg.arr, kg.arr)


LOG2E = 1.4426950408889634
RUN2_DONE = 160.0


def _softplus2(z2):
    neg_abs = lax.bitcast_convert_type(
        lax.bitcast_convert_type(z2, jnp.uint32) | jnp.uint32(0x80000000), F32)
    return jnp.maximum(z2, 0.0) + jnp.log2(1.0 + jnp.exp2(neg_abs))


def _attn_body(q_ref, k_ref, v_ref, tri_ref, o_ref, run_ref, acc_ref, *, tq, pairs):
    qi = pl.program_id(2)
    heads = range(2 * pairs)
    low = lax.broadcasted_iota(jnp.int32, (tq, LANES), 1) < HEAD_DIM

    def lane_tile(hd):
        return slice((hd // 2) * LANES, (hd // 2 + 1) * LANES)

    def q_head(hd):
        q = q_ref[0, :, lane_tile(hd)]
        zero = jnp.zeros_like(q)
        return jnp.where(low, q, zero) if hd % 2 == 0 else jnp.where(low, zero, q)

    q_heads = [q_head(hd) for hd in heads]

    def chain(hd, kb, valid, run):
        start = pl.multiple_of(kb * tq, tq)
        kblk = k_ref[0, pl.ds(start, tq), lane_tile(hd)]
        vblk = v_ref[0, pl.ds(start, tq), lane_tile(hd)]
        z2 = lax.dot_general(q_heads[hd], kblk, (((1,), (1,)), ((), ())),
                             preferred_element_type=F32) * LOG2E
        sp = _softplus2(z2)
        if valid is not None:
            sp = jnp.where(valid, sp, 0.0)
        suffix = _dot(sp.astype(BF16), tri_ref[...])
        arg = z2 - suffix
        if run is not None:
            arg = arg - jnp.concatenate([run] * (tq // LANES), axis=1)
        w = jnp.exp2(arg)
        if valid is not None:
            w = jnp.where(valid, w, 0.0)
        total = jnp.broadcast_to(suffix[:, 0:1], (tq, LANES))
        return _dot(w.astype(BF16), vblk), (total if run is None else run + total)

    def finish(accs):
        for p in range(pairs):
            o_ref[0, :, p * LANES:(p + 1) * LANES] = jnp.where(
                low, accs[2 * p], accs[2 * p + 1]).astype(BF16)

    row = lax.broadcasted_iota(jnp.int32, (tq, tq), 0)
    col = lax.broadcasted_iota(jnp.int32, (tq, tq), 1)
    below_diag = col < row

    @pl.when(qi == 0)
    def _():
        finish([chain(hd, 0, below_diag, None)[0] for hd in heads])

    @pl.when(qi > 0)
    def _():
        for hd in heads:
            pv_diag, run = chain(hd, qi, below_diag, None)
            pv_prev, run = chain(hd, qi - 1, None, run)
            acc_ref[hd] = pv_diag + pv_prev
            run_ref[hd] = run

        def more(state):
            i, min_run = state
            return jnp.logical_and(i < qi, min_run < RUN2_DONE)

        def earlier(state):
            i, _ = state
            for hd in heads:
                pv, run = chain(hd, qi - 1 - i, None, run_ref[hd])
                acc_ref[hd] += pv
                run_ref[hd] = run
            return i + 1, jnp.min(run_ref[...])

        lax.while_loop(more, earlier, (jnp.int32(1), jnp.min(run_ref[...])))
        finish([acc_ref[hd] for hd in heads])


ATTN_PAIRS = 4


def _attn_call(q, k, v, tri, *, tq):
    B, S, d = q.shape
    pairs = ATTN_PAIRS
    width = pairs * LANES
    body = functools.partial(_attn_body, tq=tq, pairs=pairs)
    qtile = pl.BlockSpec((1, tq, width), lambda b, h, i: (b, i, h))
    kvfull = pl.BlockSpec((1, S, width), lambda b, h, i: (b, 0, h))
    return pl.pallas_call(
        body,
        out_shape=jax.ShapeDtypeStruct((B, S, d), BF16),
        grid=(B, d // width, S // tq),
        in_specs=[qtile, kvfull, kvfull, _resident(_Sel(tri))],
        out_specs=qtile,
        scratch_shapes=[
            pltpu.VMEM((2 * pairs, tq, LANES), F32),
            pltpu.VMEM((2 * pairs, tq, LANES), F32),
        ],
        compiler_params=pltpu.CompilerParams(
            dimension_semantics=("arbitrary", "arbitrary", "arbitrary"),
            vmem_limit_bytes=VMEM_LIMIT),
        name="stickbreaking_attn",
    )(q, k, v, tri)


def _oproj_body(o_ref, w_ref, x_ref, out_ref):
    out_ref[0] = x_ref[0] + _dot(o_ref[0], w_ref[...])


def _oproj_call(o, w, x, *, tm):
    B, S, D = x.shape
    tile = pl.BlockSpec((1, tm, D), lambda b, s: (b, s, 0))
    otile = pl.BlockSpec((1, tm, o.shape[2]), lambda b, s: (b, s, 0))
    return pl.pallas_call(
        _oproj_body,
        out_shape=jax.ShapeDtypeStruct(x.shape, F32),
        grid=(B, S // tm),
        in_specs=[otile, _resident(w), tile],
        out_specs=tile,
        compiler_params=pltpu.CompilerParams(
            dimension_semantics=("arbitrary", "arbitrary"), vmem_limit_bytes=VMEM_LIMIT),
        name="attn_out_proj",
    )(o, w.arr, x)


FFN_CHUNK = 256


def _row(a):
    return a[:, None, :]


def kernel(x, mix_norm_g, ffn_norm_g, conv_w_in, conv_a_dw_w, conv_a_dw_b, conv_a_ln_g,
           conv_a_ln_b, conv_b_dw_w, conv_w_out, attn_w_qkv, attn_q_g, attn_k_g, attn_w_o,
           ffn_w_up, ffn_dw_w, ffn_dw_b, ffn_w_down):
    B, S, D = x.shape
    depth = mix_norm_g.shape[0]
    d_ff = ffn_w_down.shape[1]
    tm = min(512, S)
    tm_conv = min(256, S)
    tq = min(256, S)
    cw = FFN_CHUNK
    nc = d_ff // cw
    scale = HEAD_DIM ** -0.5

    tri = (lax.broadcasted_iota(jnp.int32, (tq, tq), 0)
           >= lax.broadcasted_iota(jnp.int32, (tq, tq), 1)).astype(BF16)

    def chunked(a):
        layers, rows, _ = a.shape
        return a.reshape(layers, rows, 2, nc, cw).transpose(0, 2, 3, 1, 4)

    mix_g, ffn_g = _row(mix_norm_g), _row(ffn_norm_g)
    w_in, w_out = conv_w_in.astype(BF16), conv_w_out.astype(BF16)
    a_dw_b, a_ln_g, a_ln_b = _row(conv_a_dw_b), _row(conv_a_ln_g), _row(conv_a_ln_b)
    w_qkv, w_o = attn_w_qkv.astype(BF16), attn_w_o.astype(BF16)
    q_g = _row(jnp.tile(attn_q_g * scale, (1, LANES // HEAD_DIM)))
    k_g = _row(jnp.tile(attn_k_g, (1, LANES // HEAD_DIM)))
    w_up = chunked(ffn_w_up.astype(BF16))
    dw_w = chunked(ffn_dw_w)
    dw_b = chunked(_row(ffn_dw_b))
    w_down = ffn_w_down.astype(BF16).reshape(depth, nc, cw, D)

    for layer in range(depth):
        i = layer // 2
        g = _Sel(mix_g, (layer,))
        if layer % 2 == 0:
            x = _conv_call(
                x, g, _Sel(w_in, (i,)), _Sel(conv_a_dw_w, (i,)), _Sel(a_dw_b, (i,)),
                _Sel(a_ln_g, (i,)), _Sel(a_ln_b, (i,)), _Sel(conv_b_dw_w, (i,)),
                _Sel(w_out, (i,)), tm=tm_conv)
        else:
            q, k, v = _qkv_call(x, g, _Sel(w_qkv, (i,)), _Sel(q_g, (i,)), _Sel(k_g, (i,)),
                                tm=tm)
            o = _attn_call(q, k, v, tri, tq=tq)
            x = _oproj_call(o, _Sel(w_o, (i,)), x, tm=tm)

        x = _ffn_call(
            x, _Sel(ffn_g, (layer,)),
            _Sel(w_up, (layer, 0)), _Sel(w_up, (layer, 1)),
            _Sel(dw_w, (layer, 0)), _Sel(dw_w, (layer, 1)),
            _Sel(dw_b, (layer, 0)), _Sel(dw_b, (layer, 1)),
            _Sel(w_down, (layer,)), tm=tm)
    return x
```

```python
import functools
from typing import NamedTuple

import jax
import jax.numpy as jnp
from jax import lax
from jax.experimental import pallas as pl
from jax.experimental.pallas import tpu as pltpu

EPS = 1e-6
HEAD_DIM = 64
LANES = 128
SUBLANES = 8
VMEM_LIMIT = 56 * 1024 * 1024

BF16 = jnp.bfloat16
F32 = jnp.float32


class _Sel(NamedTuple):
    arr: jax.Array
    lead: tuple = ()

    @property
    def shape(self):
        return self.arr.shape[len(self.lead):]


def _resident(op):
    n = len(op.lead)
    index = tuple(op.lead) + (0,) * (op.arr.ndim - n)
    return pl.BlockSpec((None,) * n + op.shape, lambda *_: index, pipeline_mode=pl.Buffered(1))


def _rmsnorm(x, g):
    return x * lax.rsqrt(jnp.mean(x * x, axis=-1, keepdims=True) + EPS) * g


def _dot(a, b):
    return jnp.dot(a, b, preferred_element_type=F32)


HDR = 2 * SUBLANES
GROUP = SUBLANES * SUBLANES
GROUP_ROWS = HDR + GROUP


def _ffn_body(*refs, tm, nc, attn_proj):
    if attn_proj:
        attn_ref, wo_ref, *refs = refs
    (x_ref, g_ref, wg_ref, wv_ref, dwg_ref, dwv_ref, bg_ref, bv_ref, wd_ref, o_ref, st_ref, h_ref,
     ext_g0, ext_v0, ext_g1, ext_v1, carry_g, carry_v, act_ref, *xin_scratch) = refs
    if attn_proj:
        xin, = xin_scratch
        xin[...] = x_ref[0] + _dot(attn_ref[0], wo_ref[...])
    else:
        xin = x_ref.at[0]
    assert nc % 2 == 1 and tm % GROUP == 0
    ext_g, ext_v = (ext_g0, ext_g1), (ext_v0, ext_v1)
    groups = tm // GROUP
    lane_tiles = st_ref.shape[0]

    @pl.when(pl.program_id(1) == 0)
    def _():
        carry_g[...] = jnp.zeros_like(carry_g)
        carry_v[...] = jnp.zeros_like(carry_v)

    def transposed_slab(l, g, j):
        return st_ref[l, pl.ds(g * GROUP + j, SUBLANES, stride=SUBLANES), :]

    hn = _rmsnorm(xin[...], g_ref[...])
    for l in range(lane_tiles):
        st_ref[l] = hn[:, l * LANES:(l + 1) * LANES]
    for g in range(groups):
        for j in range(0, SUBLANES, 2):
            r0 = g * GROUP + j * SUBLANES
            for l in range(lane_tiles):
                pair = [transposed_slab(l, g, j + i) for i in range(2)]
                h_ref[r0:r0 + 2 * SUBLANES, l * LANES:(l + 1) * LANES] = (
                    jnp.concatenate(pair, axis=0).astype(BF16))

    def up(c, slot):
        for w_ref, ext in ((wg_ref, ext_g[slot]), (wv_ref, ext_v[slot])):
            u = _dot(h_ref[...], w_ref[c])
            for g in range(groups):
                ext[g * GROUP_ROWS + HDR:(g + 1) * GROUP_ROWS] = u[g * GROUP:(g + 1) * GROUP]

    last_sublane = lax.broadcasted_iota(jnp.int32, (SUBLANES, 1), 0) == SUBLANES - 1

    def conv3(c, dw_ref, b_ref, ext, carry):
        dw = dw_ref[c]
        out = []
        for g in range(groups):
            base = g * GROUP_ROWS
            for i in range(2):
                tail = ext[base + GROUP + i * SUBLANES:base + GROUP + (i + 1) * SUBLANES]
                if g == 0:
                    prev = carry[c, i * SUBLANES:(i + 1) * SUBLANES]
                else:
                    prev = ext[base - HDR + i * SUBLANES:base - HDR + (i + 1) * SUBLANES]
                ext[base + i * SUBLANES:base + (i + 1) * SUBLANES] = pltpu.roll(
                    jnp.where(last_sublane, prev, tail), 1, axis=0)
            out.append(dw[2:3] * ext[base + HDR:base + HDR + GROUP]
                       + dw[1:2] * ext[base + SUBLANES:base + SUBLANES + GROUP]
                       + dw[0:1] * ext[base:base + GROUP] + b_ref[c])
        carry[c] = ext[groups * GROUP_ROWS - HDR:groups * GROUP_ROWS]
        return jnp.concatenate(out, axis=0)

    def activate(c, slot):
        gate = conv3(c, dwg_ref, bg_ref, ext_g[slot], carry_g)
        val = conv3(c, dwv_ref, bv_ref, ext_v[slot], carry_v)
        act_ref[c] = (gate * jax.nn.sigmoid(gate) * val).astype(BF16)

    def chunk_pair(j, carry):
        c = 2 * j
        activate(c, 0)
        up(c + 1, 1)
        activate(c + 1, 1)
        up(c + 2, 0)
        return carry

    up(0, 0)
    lax.fori_loop(0, (nc - 1) // 2, chunk_pair, 0)
    activate(nc - 1, 0)

    acc = _dot(act_ref[0], wd_ref[0])
    for c in range(1, nc):
        acc = acc + _dot(act_ref[c], wd_ref[c])
    for l in range(lane_tiles):
        st_ref[l] = acc[:, l * LANES:(l + 1) * LANES]
    for g in range(groups):
        for s in range(SUBLANES):
            rows = slice(g * GROUP + s * SUBLANES, g * GROUP + (s + 1) * SUBLANES)
            for l in range(lane_tiles):
                lanes = slice(l * LANES, (l + 1) * LANES)
                o_ref[0, rows, lanes] = xin[rows, lanes] + transposed_slab(l, g, s)


def _ffn_call(x, g, wg, wv, dwg, dwv, bg, bv, wd, *, tm, attn=None, wo=None):
    B, S, D = x.shape
    nc, _, cw = wg.shape
    attn_proj = attn is not None
    body = functools.partial(_ffn_body, tm=tm, nc=nc, attn_proj=attn_proj)
    tile = pl.BlockSpec((1, tm, D), lambda b, s: (b, s, 0))
    consts = (g, wg, wv, dwg, dwv, bg, bv, wd)
    lead_specs, lead_args, tail_scratch = [], [], []
    if attn_proj:
        lead_specs = [pl.BlockSpec((1, tm, attn.shape[2]), lambda b, s: (b, s, 0)), _resident(wo)]
        lead_args = [attn, wo.arr]
        tail_scratch = [pltpu.VMEM((tm, D), F32)]
    return pl.pallas_call(
        body,
        out_shape=jax.ShapeDtypeStruct(x.shape, F32),
        grid=(B, S // tm),
        in_specs=lead_specs + [tile] + [_resident(c) for c in consts],
        out_specs=tile,
        scratch_shapes=[
            pltpu.VMEM((D // LANES, tm, LANES), F32),
            pltpu.VMEM((tm, D), BF16),
            pltpu.VMEM((tm // GROUP * GROUP_ROWS, cw), F32),
            pltpu.VMEM((tm // GROUP * GROUP_ROWS, cw), F32),
            pltpu.VMEM((tm // GROUP * GROUP_ROWS, cw), F32),
            pltpu.VMEM((tm // GROUP * GROUP_ROWS, cw), F32),
            pltpu.VMEM((nc, HDR, cw), F32),
            pltpu.VMEM((nc, HDR, cw), F32),
            pltpu.VMEM((nc, tm, cw), BF16),
        ] + tail_scratch,
        compiler_params=pltpu.CompilerParams(
            dimension_semantics=("arbitrary", "arbitrary"), vmem_limit_bytes=VMEM_LIMIT),
        name="conv_ffn",
    )(*lead_args, x, *[c.arr for c in consts])


A_HALO = 32
A_ROWS = 64
LN_ROWS = 32


def _conv_body(x_ref, g_ref, win_ref, adw_ref, adb_ref, lng_ref, lnb_ref, bdw_ref, wout_ref,
               o_ref, h_ref, ext_a, ext_b, conv_ref, cat_ref, *, tm, da, ka):
    @pl.when(pl.program_id(1) == 0)
    def _():
        ext_a[0:A_HALO] = jnp.zeros((A_HALO, da), F32)
        ext_b[0:SUBLANES] = jnp.zeros((SUBLANES, da), F32)

    x = x_ref[0]
    h_ref[...] = _rmsnorm(x, g_ref[...]).astype(BF16)

    def proj(i):
        return _dot(h_ref[...], win_ref[:, i * da:(i + 1) * da])

    ext_a[A_HALO:A_HALO + tm] = proj(0) * jax.nn.sigmoid(proj(1))
    base = A_HALO - (ka - 1)
    for r0 in range(0, tm, A_ROWS):
        for l0 in range(0, da, LANES):
            lanes = slice(l0, l0 + LANES)
            acc = jnp.broadcast_to(adb_ref[:, lanes], (A_ROWS, LANES))
            for phase in range(SUBLANES):
                rows = A_ROWS if phase == 0 else A_ROWS + SUBLANES
                part = None
                for k in range(ka):
                    if (base + k) % SUBLANES != phase:
                        continue
                    start = r0 + (base + k) - phase
                    term = adw_ref[k:k + 1, lanes] * ext_a[start:start + rows, lanes]
                    part = term if part is None else part + term
                acc = acc + part[phase:phase + A_ROWS]
            conv_ref[r0:r0 + A_ROWS, lanes] = acc
    for r0 in range(0, tm, LN_ROWS):
        acc = conv_ref[r0:r0 + LN_ROWS]
        mu = jnp.mean(acc, axis=-1, keepdims=True)
        xc = acc - mu
        y = xc * lax.rsqrt(jnp.mean(xc * xc, axis=-1, keepdims=True) + EPS)
        y = y * lng_ref[...] + lnb_ref[...]
        cat_ref[r0:r0 + LN_ROWS, 0:da] = (y * jax.nn.sigmoid(y)).astype(BF16)
    ext_a[0:A_HALO] = ext_a[tm:tm + A_HALO]

    m = proj(3) * proj(4)
    ext_b[SUBLANES:SUBLANES + tm] = m
    bdw = bdw_ref[...]
    y = (bdw[2:3] * m + bdw[1:2] * ext_b[SUBLANES - 1:SUBLANES - 1 + tm]
         + bdw[0:1] * ext_b[SUBLANES - 2:SUBLANES - 2 + tm])
    cat_ref[:, da:2 * da] = (proj(2) * y).astype(BF16)
    ext_b[0:SUBLANES] = m[tm - SUBLANES:tm]

    o_ref[0] = x + _dot(cat_ref[...], wout_ref[...])


def _conv_call(x, g, win, adw, adb, lng, lnb, bdw, wout, *, tm):
    B, S, D = x.shape
    ka, da = adw.shape
    assert ka - 1 <= A_HALO and tm % A_ROWS == 0 and tm % LN_ROWS == 0
    body = functools.partial(_conv_body, tm=tm, da=da, ka=ka)
    tile = pl.BlockSpec((1, tm, D), lambda b, s: (b, s, 0))
    consts = (g, win, adw, adb, lng, lnb, bdw, wout)
    return pl.pallas_call(
        body,
        out_shape=jax.ShapeDtypeStruct(x.shape, F32),
        grid=(B, S // tm),
        in_specs=[tile] + [_resident(c) for c in consts],
        out_specs=tile,
        scratch_shapes=[
            pltpu.VMEM((tm, D), BF16),
            pltpu.VMEM((tm + A_HALO, da), F32),
            pltpu.VMEM((tm + SUBLANES, da), F32),
            pltpu.VMEM((tm, da), F32),
            pltpu.VMEM((tm, 2 * da), BF16),
        ],
        compiler_params=pltpu.CompilerParams(
            dimension_semantics=("arbitrary", "arbitrary"), vmem_limit_bytes=VMEM_LIMIT),
        name="conv_mixer",
    )(x, *[c.arr for c in consts])


QKV_COLS = 256


def _qkv_body(x_ref, g_ref, w_ref, qg_ref, kg_ref, q_ref, k_ref, v_ref, h_ref, *, tm, d):
    x = x_ref[0]
    h_ref[...] = _rmsnorm(x, g_ref[...]).astype(BF16)
    low = lax.broadcasted_iota(jnp.int32, (tm, LANES), 1) < HEAD_DIM

    def head_norm(t, gain):
        t2 = t * t
        s_lo = jnp.sum(jnp.where(low, t2, 0.0), axis=-1, keepdims=True)
        s_hi = jnp.sum(jnp.where(low, 0.0, t2), axis=-1, keepdims=True)
        ms = jnp.where(low, s_lo, s_hi) * (1.0 / HEAD_DIM)
        return t * lax.rsqrt(ms + EPS) * gain

    for part, (out_ref, gain_ref) in enumerate(((q_ref, qg_ref), (k_ref, kg_ref), (v_ref, None))):
        for c0 in range(0, d, QKV_COLS):
            y = _dot(h_ref[...], w_ref[:, part * d + c0:part * d + c0 + QKV_COLS])
            for l0 in range(0, QKV_COLS, LANES):
                t = y[:, l0:l0 + LANES]
                if gain_ref is not None:
                    t = head_norm(t, gain_ref[...])
                out_ref[0, :, c0 + l0:c0 + l0 + LANES] = t.astype(BF16)


def _qkv_call(x, g, w, qg, kg, *, tm):
    B, S, D = x.shape
    d = w.shape[1] // 3
    body = functools.partial(_qkv_body, tm=tm, d=d)
    tile = pl.BlockSpec((1, tm, D), lambda b, s: (b, s, 0))
    otile = pl.BlockSpec((1, tm, d), lambda b, s: (b, s, 0))
    out = jax.ShapeDtypeStruct((B, S, d), BF16)
    return pl.pallas_call(
        body,
        out_shape=(out, out, out),
        grid=(B, S // tm),
        in_specs=[tile, _resident(g), _resident(w), _resident(qg), _resident(kg)],
        out_specs=(otile, otile, otile),
        scratch_shapes=[pltpu.VMEM((tm, D), BF16)],
        compiler_params=pltpu.CompilerParams(
            dimension_semantics=("arbitrary", "arbitrary"), vmem_limit_bytes=VMEM_LIMIT),
        name="qkv_proj",
    )(x, g.arr, w.arr, qg.arr, kg.arr)


LOG2E = 1.4426950408889634
RUN2_DONE = 160.0


def _softplus2(z2):
    neg_abs = lax.bitcast_convert_type(
        lax.bitcast_convert_type(z2, jnp.uint32) | jnp.uint32(0x80000000), F32)
    return jnp.maximum(z2, 0.0) + jnp.log2(1.0 + jnp.exp2(neg_abs))


def _attn_body(q_ref, k_ref, v_ref, tri_ref, o_ref, run_ref, acc_ref, *, tq, pairs):
    qi = pl.program_id(2)
    heads = range(2 * pairs)
    low = lax.broadcasted_iota(jnp.int32, (tq, LANES), 1) < HEAD_DIM

    def lane_tile(hd):
        return slice((hd // 2) * LANES, (hd // 2 + 1) * LANES)

    def q_head(hd):
        q = q_ref[0, :, lane_tile(hd)]
        zero = jnp.zeros_like(q)
        return jnp.where(low, q, zero) if hd % 2 == 0 else jnp.where(low, zero, q)

    q_heads = [q_head(hd) for hd in heads]

    def chain(hd, kb, valid, run):
        start = pl.multiple_of(kb * tq, tq)
        kblk = k_ref[0, pl.ds(start, tq), lane_tile(hd)]
        vblk = v_ref[0, pl.ds(start, tq), lane_tile(hd)]
        z2 = lax.dot_general(q_heads[hd], kblk, (((1,), (1,)), ((), ())),
                             preferred_element_type=F32) * LOG2E
        sp = _softplus2(z2)
        if valid is not None:
            sp = jnp.where(valid, sp, 0.0)
        suffix = _dot(sp.astype(BF16), tri_ref[...])
        arg = z2 - suffix
        if run is not None:
            arg = arg - jnp.concatenate([run] * (tq // LANES), axis=1)
        w = jnp.exp2(arg)
        if valid is not None:
            w = jnp.where(valid, w, 0.0)
        total = jnp.broadcast_to(suffix[:, 0:1], (tq, LANES))
        return _dot(w.astype(BF16), vblk), (total if run is None else run + total)

    def finish(accs):
        for p in range(pairs):
            o_ref[0, :, p * LANES:(p + 1) * LANES] = jnp.where(
                low, accs[2 * p], accs[2 * p + 1]).astype(BF16)

    row = lax.broadcasted_iota(jnp.int32, (tq, tq), 0)
    col = lax.broadcasted_iota(jnp.int32, (tq, tq), 1)
    below_diag = col < row

    @pl.when(qi == 0)
    def _():
        finish([chain(hd, 0, below_diag, None)[0] for hd in heads])

    @pl.when(qi > 0)
    def _():
        for hd in heads:
            pv_diag, run = chain(hd, qi, below_diag, None)
            pv_prev, run = chain(hd, qi - 1, None, run)
            acc_ref[hd] = pv_diag + pv_prev
            run_ref[hd] = run

        def more(state):
            i, min_run = state
            return jnp.logical_and(i < qi, min_run < RUN2_DONE)

        def earlier(state):
            i, _ = state
            for hd in heads:
                pv, run = chain(hd, qi - 1 - i, None, run_ref[hd])
                acc_ref[hd] += pv
                run_ref[hd] = run
            return i + 1, jnp.min(run_ref[...])

        lax.while_loop(more, earlier, (jnp.int32(1), jnp.min(run_ref[...])))
        finish([acc_ref[hd] for hd in heads])


ATTN_PAIRS = 4


def _attn_call(q, k, v, tri, *, tq):
    B, S, d = q.shape
    pairs = ATTN_PAIRS
    width = pairs * LANES
    body = functools.partial(_attn_body, tq=tq, pairs=pairs)
    qtile = pl.BlockSpec((1, tq, width), lambda b, h, i: (b, i, h))
    kvfull = pl.BlockSpec((1, S, width), lambda b, h, i: (b, 0, h))
    return pl.pallas_call(
        body,
        out_shape=jax.ShapeDtypeStruct((B, S, d), BF16),
        grid=(B, d // width, S // tq),
        in_specs=[qtile, kvfull, kvfull, _resident(_Sel(tri))],
        out_specs=qtile,
        scratch_shapes=[
            pltpu.VMEM((2 * pairs, tq, LANES), F32),
            pltpu.VMEM((2 * pairs, tq, LANES), F32),
        ],
        compiler_params=pltpu.CompilerParams(
            dimension_semantics=("arbitrary", "arbitrary", "arbitrary"),
            vmem_limit_bytes=VMEM_LIMIT),
        name="stickbreaking_attn",
    )(q, k, v, tri)


FFN_CHUNK = 256


def _row(a):
    return a[:, None, :]


def kernel(x, mix_norm_g, ffn_norm_g, conv_w_in, conv_a_dw_w, conv_a_dw_b, conv_a_ln_g,
           conv_a_ln_b, conv_b_dw_w, conv_w_out, attn_w_qkv, attn_q_g, attn_k_g, attn_w_o,
           ffn_w_up, ffn_dw_w, ffn_dw_b, ffn_w_down):
    B, S, D = x.shape
    depth = mix_norm_g.shape[0]
    d_ff = ffn_w_down.shape[1]
    tm = min(512, S)
    tm_conv = min(512, S)
    tq = min(256, S)
    cw = FFN_CHUNK
    nc = d_ff // cw
    scale = HEAD_DIM ** -0.5

    tri = (lax.broadcasted_iota(jnp.int32, (tq, tq), 0)
           >= lax.broadcasted_iota(jnp.int32, (tq, tq), 1)).astype(BF16)

    def chunked(a):
        layers, rows, _ = a.shape
        return a.reshape(layers, rows, 2, nc, cw).transpose(0, 2, 3, 1, 4)

    mix_g, ffn_g = _row(mix_norm_g), _row(ffn_norm_g)
    w_in, w_out = conv_w_in.astype(BF16), conv_w_out.astype(BF16)
    a_dw_b, a_ln_g, a_ln_b = _row(conv_a_dw_b), _row(conv_a_ln_g), _row(conv_a_ln_b)
    w_qkv, w_o = attn_w_qkv.astype(BF16), attn_w_o.astype(BF16)
    q_g = _row(jnp.tile(attn_q_g * scale, (1, LANES // HEAD_DIM)))
    k_g = _row(jnp.tile(attn_k_g, (1, LANES // HEAD_DIM)))
    w_up = chunked(ffn_w_up.astype(BF16))
    dw_w = chunked(ffn_dw_w)
    dw_b = chunked(_row(ffn_dw_b))
    w_down = ffn_w_down.astype(BF16).reshape(depth, nc, cw, D)

    for layer in range(depth):
        i = layer // 2
        g = _Sel(mix_g, (layer,))
        mixer_out = {}
        if layer % 2 == 0:
            x = _conv_call(
                x, g, _Sel(w_in, (i,)), _Sel(conv_a_dw_w, (i,)), _Sel(a_dw_b, (i,)),
                _Sel(a_ln_g, (i,)), _Sel(a_ln_b, (i,)), _Sel(conv_b_dw_w, (i,)),
                _Sel(w_out, (i,)), tm=tm_conv)
        else:
            q, k, v = _qkv_call(x, g, _Sel(w_qkv, (i,)), _Sel(q_g, (i,)), _Sel(k_g, (i,)),
                                tm=tm)
            mixer_out = dict(attn=_attn_call(q, k, v, tri, tq=tq), wo=_Sel(w_o, (i,)))

        x = _ffn_call(
            x, _Sel(ffn_g, (layer,)),
            _Sel(w_up, (layer, 0)), _Sel(w_up, (layer, 1)),
            _Sel(dw_w, (layer, 0)), _Sel(dw_w, (layer, 1)),
            _Sel(dw_b, (layer, 0)), _Sel(dw_b, (layer, 1)),
            _Sel(w_down, (layer,)), tm=tm, **mixer_out)
    return x
```

```python
import functools
from typing import NamedTuple

import jax
import jax.numpy as jnp
from jax import lax
from jax.experimental import pallas as pl
from jax.experimental.pallas import tpu as pltpu

EPS = 1e-6
HEAD_DIM = 64
LANES = 128
SUBLANES = 8
VMEM_LIMIT = 56 * 1024 * 1024

BF16 = jnp.bfloat16
F32 = jnp.float32


class _Sel(NamedTuple):
    arr: jax.Array
    lead: tuple = ()

    @property
    def shape(self):
        return self.arr.shape[len(self.lead):]


def _resident(op):
    n = len(op.lead)
    index = tuple(op.lead) + (0,) * (op.arr.ndim - n)
    return pl.BlockSpec((None,) * n + op.shape, lambda *_: index, pipeline_mode=pl.Buffered(1))


def _rmsnorm(x, g):
    return x * lax.rsqrt(jnp.mean(x * x, axis=-1, keepdims=True) + EPS) * g


def _dot(a, b):
    return jnp.dot(a, b, preferred_element_type=F32)


HDR = 2 * SUBLANES
GROUP = SUBLANES * SUBLANES
GROUP_ROWS = HDR + GROUP


def _ffn_body(*refs, tm, nc, attn_proj):
    if attn_proj:
        attn_ref, wo_ref, *refs = refs
    (x_ref, g_ref, wg_ref, wv_ref, dwg_ref, dwv_ref, bg_ref, bv_ref, wd_ref, o_ref, st_ref, h_ref,
     ext_g0, ext_v0, ext_g1, ext_v1, carry_g, carry_v, act_ref, *xin_scratch) = refs
    if attn_proj:
        xin, = xin_scratch
        xin[...] = x_ref[0] + _dot(attn_ref[0], wo_ref[...])
    else:
        xin = x_ref.at[0]
    assert nc % 2 == 1 and tm % GROUP == 0
    ext_g, ext_v = (ext_g0, ext_g1), (ext_v0, ext_v1)
    groups = tm // GROUP
    lane_tiles = st_ref.shape[0]

    @pl.when(pl.program_id(1) == 0)
    def _():
        carry_g[...] = jnp.zeros_like(carry_g)
        carry_v[...] = jnp.zeros_like(carry_v)

    def transposed_slab(l, g, j):
        return st_ref[l, pl.ds(g * GROUP + j, SUBLANES, stride=SUBLANES), :]

    hn = _rmsnorm(xin[...], g_ref[...])
    for l in range(lane_tiles):
        st_ref[l] = hn[:, l * LANES:(l + 1) * LANES]
    for g in range(groups):
        for j in range(0, SUBLANES, 2):
            r0 = g * GROUP + j * SUBLANES
            for l in range(lane_tiles):
                pair = [transposed_slab(l, g, j + i) for i in range(2)]
                h_ref[r0:r0 + 2 * SUBLANES, l * LANES:(l + 1) * LANES] = (
                    jnp.concatenate(pair, axis=0).astype(BF16))

    def up(c, slot):
        for w_ref, ext in ((wg_ref, ext_g[slot]), (wv_ref, ext_v[slot])):
            u = _dot(h_ref[...], w_ref[c])
            for g in range(groups):
                ext[g * GROUP_ROWS + HDR:(g + 1) * GROUP_ROWS] = u[g * GROUP:(g + 1) * GROUP]

    last_sublane = lax.broadcasted_iota(jnp.int32, (SUBLANES, 1), 0) == SUBLANES - 1

    def conv3(c, dw_ref, b_ref, ext, carry):
        dw = dw_ref[c]
        out = []
        for g in range(groups):
            base = g * GROUP_ROWS
            for i in range(2):
                tail = ext[base + GROUP + i * SUBLANES:base + GROUP + (i + 1) * SUBLANES]
                if g == 0:
                    prev = carry[c, i * SUBLANES:(i + 1) * SUBLANES]
                else:
                    prev = ext[base - HDR + i * SUBLANES:base - HDR + (i + 1) * SUBLANES]
                ext[base + i * SUBLANES:base + (i + 1) * SUBLANES] = pltpu.roll(
                    jnp.where(last_sublane, prev, tail), 1, axis=0)
            out.append(dw[2:3] * ext[base + HDR:base + HDR + GROUP]
                       + dw[1:2] * ext[base + SUBLANES:base + SUBLANES + GROUP]
                       + dw[0:1] * ext[base:base + GROUP] + b_ref[c])
        carry[c] = ext[groups * GROUP_ROWS - HDR:groups * GROUP_ROWS]
        return jnp.concatenate(out, axis=0)

    def activate(c, slot):
        gate = conv3(c, dwg_ref, bg_ref, ext_g[slot], carry_g)
        val = conv3(c, dwv_ref, bv_ref, ext_v[slot], carry_v)
        act_ref[c] = (gate * jax.nn.sigmoid(gate) * val).astype(BF16)

    def chunk_pair(j, carry):
        c = 2 * j
        activate(c, 0)
        up(c + 1, 1)
        activate(c + 1, 1)
        up(c + 2, 0)
        return carry

    up(0, 0)
    lax.fori_loop(0, (nc - 1) // 2, chunk_pair, 0)
    activate(nc - 1, 0)

    acc = _dot(act_ref[0], wd_ref[0])
    for c in range(1, nc):
        acc = acc + _dot(act_ref[c], wd_ref[c])
    for l in range(lane_tiles):
        st_ref[l] = acc[:, l * LANES:(l + 1) * LANES]
    for g in range(groups):
        for s in range(SUBLANES):
            rows = slice(g * GROUP + s * SUBLANES, g * GROUP + (s + 1) * SUBLANES)
            for l in range(lane_tiles):
                lanes = slice(l * LANES, (l + 1) * LANES)
                o_ref[0, rows, lanes] = xin[rows, lanes] + transposed_slab(l, g, s)


def _ffn_call(x, g, wg, wv, dwg, dwv, bg, bv, wd, *, tm, attn=None, wo=None):
    B, S, D = x.shape
    nc, _, cw = wg.shape
    attn_proj = attn is not None
    body = functools.partial(_ffn_body, tm=tm, nc=nc, attn_proj=attn_proj)
    tile = pl.BlockSpec((1, tm, D), lambda b, s: (b, s, 0))
    consts = (g, wg, wv, dwg, dwv, bg, bv, wd)
    lead_specs, lead_args, tail_scratch = [], [], []
    if attn_proj:
        lead_specs = [pl.BlockSpec((1, tm, attn.shape[2]), lambda b, s: (b, s, 0)), _resident(wo)]
        lead_args = [attn, wo.arr]
        tail_scratch = [pltpu.VMEM((tm, D), F32)]
    return pl.pallas_call(
        body,
        out_shape=jax.ShapeDtypeStruct(x.shape, F32),
        grid=(B, S // tm),
        in_specs=lead_specs + [tile] + [_resident(c) for c in consts],
        out_specs=tile,
        scratch_shapes=[
            pltpu.VMEM((D // LANES, tm, LANES), F32),
            pltpu.VMEM((tm, D), BF16),
            pltpu.VMEM((tm // GROUP * GROUP_ROWS, cw), F32),
            pltpu.VMEM((tm // GROUP * GROUP_ROWS, cw), F32),
            pltpu.VMEM((tm // GROUP * GROUP_ROWS, cw), F32),
            pltpu.VMEM((tm // GROUP * GROUP_ROWS, cw), F32),
            pltpu.VMEM((nc, HDR, cw), F32),
            pltpu.VMEM((nc, HDR, cw), F32),
            pltpu.VMEM((nc, tm, cw), BF16),
        ] + tail_scratch,
        compiler_params=pltpu.CompilerParams(
            dimension_semantics=("arbitrary", "arbitrary"), vmem_limit_bytes=VMEM_LIMIT),
        name="conv_ffn",
    )(*lead_args, x, *[c.arr for c in consts])


A_HALO = 32
A_ROWS = 64
LN_ROWS = 32


def _conv_body(x_ref, g_ref, win_ref, adw_ref, adb_ref, lng_ref, lnb_ref, bdw_ref, wout_ref,
               o_ref, h_ref, ext_a, ext_b, conv_ref, cat_ref, *, tm, da, ka):
    @pl.when(pl.program_id(1) == 0)
    def _():
        ext_a[0:A_HALO] = jnp.zeros((A_HALO, da), F32)
        ext_b[0:SUBLANES] = jnp.zeros((SUBLANES, da), F32)

    x = x_ref[0]
    h_ref[...] = _rmsnorm(x, g_ref[...]).astype(BF16)

    def pair(off, n):
        y = _dot(h_ref[...], win_ref[:, off + 2 * n * LANES:off + 2 * (n + 1) * LANES])
        return y[:, :LANES], y[:, LANES:]

    for n in range(da // LANES):
        val, gate = pair(0, n)
        ext_a[A_HALO:A_HALO + tm, n * LANES:(n + 1) * LANES] = val * jax.nn.sigmoid(gate)
    base = A_HALO - (ka - 1)
    for r0 in range(0, tm, A_ROWS):
        for l0 in range(0, da, LANES):
            lanes = slice(l0, l0 + LANES)
            acc = jnp.broadcast_to(adb_ref[:, lanes], (A_ROWS, LANES))
            for phase in range(SUBLANES):
                rows = A_ROWS if phase == 0 else A_ROWS + SUBLANES
                part = None
                for k in range(ka):
                    if (base + k) % SUBLANES != phase:
                        continue
                    start = r0 + (base + k) - phase
                    term = adw_ref[k:k + 1, lanes] * ext_a[start:start + rows, lanes]
                    part = term if part is None else part + term
                acc = acc + part[phase:phase + A_ROWS]
            conv_ref[r0:r0 + A_ROWS, lanes] = acc
    for r0 in range(0, tm, LN_ROWS):
        acc = conv_ref[r0:r0 + LN_ROWS]
        mu = jnp.mean(acc, axis=-1, keepdims=True)
        xc = acc - mu
        y = xc * lax.rsqrt(jnp.mean(xc * xc, axis=-1, keepdims=True) + EPS)
        y = y * lng_ref[...] + lnb_ref[...]
        cat_ref[r0:r0 + LN_ROWS, 0:da] = (y * jax.nn.sigmoid(y)).astype(BF16)
    ext_a[0:A_HALO] = ext_a[tm:tm + A_HALO]

    for n in range(da // LANES):
        gc, bh = pair(3 * da, n)
        ext_b[SUBLANES:SUBLANES + tm, n * LANES:(n + 1) * LANES] = gc * bh
    bdw = bdw_ref[...]
    y = (bdw[2:3] * ext_b[SUBLANES:SUBLANES + tm]
         + bdw[1:2] * ext_b[SUBLANES - 1:SUBLANES - 1 + tm]
         + bdw[0:1] * ext_b[SUBLANES - 2:SUBLANES - 2 + tm])
    gb = _dot(h_ref[...], win_ref[:, 2 * da:3 * da])
    cat_ref[:, da:2 * da] = (gb * y).astype(BF16)
    ext_b[0:SUBLANES] = ext_b[tm:tm + SUBLANES]

    o_ref[0] = x + _dot(cat_ref[...], wout_ref[...])


def _conv_call(x, g, win, adw, adb, lng, lnb, bdw, wout, *, tm):
    B, S, D = x.shape
    ka, da = adw.shape
    assert ka - 1 <= A_HALO and tm % A_ROWS == 0 and tm % LN_ROWS == 0
    body = functools.partial(_conv_body, tm=tm, da=da, ka=ka)
    tile = pl.BlockSpec((1, tm, D), lambda b, s: (b, s, 0))
    consts = (g, win, adw, adb, lng, lnb, bdw, wout)
    return pl.pallas_call(
        body,
        out_shape=jax.ShapeDtypeStruct(x.shape, F32),
        grid=(B, S // tm),
        in_specs=[tile] + [_resident(c) for c in consts],
        out_specs=tile,
        scratch_shapes=[
            pltpu.VMEM((tm, D), BF16),
            pltpu.VMEM((tm + A_HALO, da), F32),
            pltpu.VMEM((tm + SUBLANES, da), F32),
            pltpu.VMEM((tm, da), F32),
            pltpu.VMEM((tm, 2 * da), BF16),
        ],
        compiler_params=pltpu.CompilerParams(
            dimension_semantics=("arbitrary", "arbitrary"), vmem_limit_bytes=VMEM_LIMIT),
        name="conv_mixer",
    )(x, *[c.arr for c in consts])


QKV_COLS = 256


def _qkv_body(x_ref, g_ref, w_ref, qg_ref, kg_ref, q_ref, k_ref, v_ref, h_ref, *, tm, d):
    x = x_ref[0]
    h_ref[...] = _rmsnorm(x, g_ref[...]).astype(BF16)
    low = lax.broadcasted_iota(jnp.int32, (tm, LANES), 1) < HEAD_DIM

    def head_norm(t, gain):
        t2 = t * t
        s_lo = jnp.sum(jnp.where(low, t2, 0.0), axis=-1, keepdims=True)
        s_hi = jnp.sum(jnp.where(low, 0.0, t2), axis=-1, keepdims=True)
        ms = jnp.where(low, s_lo, s_hi) * (1.0 / HEAD_DIM)
        return t * lax.rsqrt(ms + EPS) * gain

    for part, (out_ref, gain_ref) in enumerate(((q_ref, qg_ref), (k_ref, kg_ref), (v_ref, None))):
        for c0 in range(0, d, QKV_COLS):
            y = _dot(h_ref[...], w_ref[:, part * d + c0:part * d + c0 + QKV_COLS])
            for l0 in range(0, QKV_COLS, LANES):
                t = y[:, l0:l0 + LANES]
                if gain_ref is not None:
                    t = head_norm(t, gain_ref[...])
                out_ref[0, :, c0 + l0:c0 + l0 + LANES] = t.astype(BF16)


def _qkv_call(x, g, w, qg, kg, *, tm):
    B, S, D = x.shape
    d = w.shape[1] // 3
    body = functools.partial(_qkv_body, tm=tm, d=d)
    tile = pl.BlockSpec((1, tm, D), lambda b, s: (b, s, 0))
    otile = pl.BlockSpec((1, tm, d), lambda b, s: (b, s, 0))
    out = jax.ShapeDtypeStruct((B, S, d), BF16)
    return pl.pallas_call(
        body,
        out_shape=(out, out, out),
        grid=(B, S // tm),
        in_specs=[tile, _resident(g), _resident(w), _resident(qg), _resident(kg)],
        out_specs=(otile, otile, otile),
        scratch_shapes=[pltpu.VMEM((tm, D), BF16)],
        compiler_params=pltpu.CompilerParams(
            dimension_semantics=("arbitrary", "arbitrary"), vmem_limit_bytes=VMEM_LIMIT),
        name="qkv_proj",
    )(x, g.arr, w.arr, qg.arr, kg.arr)


LOG2E = 1.4426950408889634
RUN2_DONE = 160.0


def _softplus2(z2):
    neg_abs = lax.bitcast_convert_type(
        lax.bitcast_convert_type(z2, jnp.uint32) | jnp.uint32(0x80000000), F32)
    return jnp.maximum(z2, 0.0) + jnp.log2(1.0 + jnp.exp2(neg_abs))


def _attn_body(q_ref, k_ref, v_ref, tri_ref, o_ref, run_ref, acc_ref, *, tq, pairs):
    qi = pl.program_id(2)
    heads = range(2 * pairs)
    low = lax.broadcasted_iota(jnp.int32, (tq, LANES), 1) < HEAD_DIM

    def lane_tile(hd):
        return slice((hd // 2) * LANES, (hd // 2 + 1) * LANES)

    def q_head(hd):
        q = q_ref[0, :, lane_tile(hd)]
        zero = jnp.zeros_like(q)
        return jnp.where(low, q, zero) if hd % 2 == 0 else jnp.where(low, zero, q)

    q_heads = [q_head(hd) for hd in heads]

    def chain(hd, kb, valid, run):
        start = pl.multiple_of(kb * tq, tq)
        kblk = k_ref[0, pl.ds(start, tq), lane_tile(hd)]
        vblk = v_ref[0, pl.ds(start, tq), lane_tile(hd)]
        z2 = lax.dot_general(q_heads[hd], kblk, (((1,), (1,)), ((), ())),
                             preferred_element_type=F32) * LOG2E
        sp = _softplus2(z2)
        if valid is not None:
            sp = jnp.where(valid, sp, 0.0)
        suffix = _dot(sp.astype(BF16), tri_ref[...])
        arg = z2 - suffix
        if run is not None:
            arg = arg - jnp.concatenate([run] * (tq // LANES), axis=1)
        w = jnp.exp2(arg)
        if valid is not None:
            w = jnp.where(valid, w, 0.0)
        total = jnp.broadcast_to(suffix[:, 0:1], (tq, LANES))
        return _dot(w.astype(BF16), vblk), (total if run is None else run + total)

    def finish(accs):
        for p in range(pairs):
            o_ref[0, :, p * LANES:(p + 1) * LANES] = jnp.where(
                low, accs[2 * p], accs[2 * p + 1]).astype(BF16)

    row = lax.broadcasted_iota(jnp.int32, (tq, tq), 0)
    col = lax.broadcasted_iota(jnp.int32, (tq, tq), 1)
    below_diag = col < row

    @pl.when(qi == 0)
    def _():
        finish([chain(hd, 0, below_diag, None)[0] for hd in heads])

    @pl.when(qi > 0)
    def _():
        for hd in heads:
            pv_diag, run = chain(hd, qi, below_diag, None)
            pv_prev, run = chain(hd, qi - 1, None, run)
            acc_ref[hd] = pv_diag + pv_prev
            run_ref[hd] = run

        def more(state):
            i, min_run = state
            return jnp.logical_and(i < qi, min_run < RUN2_DONE)

        def earlier(state):
            i, _ = state
            for hd in heads:
                pv, run = chain(hd, qi - 1 - i, None, run_ref[hd])
                acc_ref[hd] += pv
                run_ref[hd] = run
            return i + 1, jnp.min(run_ref[...])

        lax.while_loop(more, earlier, (jnp.int32(1), jnp.min(run_ref[...])))
        finish([acc_ref[hd] for hd in heads])


ATTN_PAIRS = 4


def _attn_call(q, k, v, tri, *, tq):
    B, S, d = q.shape
    pairs = ATTN_PAIRS
    width = pairs * LANES
    body = functools.partial(_attn_body, tq=tq, pairs=pairs)
    qtile = pl.BlockSpec((1, tq, width), lambda b, h, i: (b, i, h))
    kvfull = pl.BlockSpec((1, S, width), lambda b, h, i: (b, 0, h))
    return pl.pallas_call(
        body,
        out_shape=jax.ShapeDtypeStruct((B, S, d), BF16),
        grid=(B, d // width, S // tq),
        in_specs=[qtile, kvfull, kvfull, _resident(_Sel(tri))],
        out_specs=qtile,
        scratch_shapes=[
            pltpu.VMEM((2 * pairs, tq, LANES), F32),
            pltpu.VMEM((2 * pairs, tq, LANES), F32),
        ],
        compiler_params=pltpu.CompilerParams(
            dimension_semantics=("arbitrary", "arbitrary", "arbitrary"),
            vmem_limit_bytes=VMEM_LIMIT),
        name="stickbreaking_attn",
    )(q, k, v, tri)


FFN_CHUNK = 256


def _row(a):
    return a[:, None, :]


def kernel(x, mix_norm_g, ffn_norm_g, conv_w_in, conv_a_dw_w, conv_a_dw_b, conv_a_ln_g,
           conv_a_ln_b, conv_b_dw_w, conv_w_out, attn_w_qkv, attn_q_g, attn_k_g, attn_w_o,
           ffn_w_up, ffn_dw_w, ffn_dw_b, ffn_w_down):
    B, S, D = x.shape
    depth = mix_norm_g.shape[0]
    d_ff = ffn_w_down.shape[1]
    tm = min(512, S)
    tm_conv = min(512, S)
    tq = min(256, S)
    cw = FFN_CHUNK
    nc = d_ff // cw
    scale = HEAD_DIM ** -0.5

    tri = (lax.broadcasted_iota(jnp.int32, (tq, tq), 0)
           >= lax.broadcasted_iota(jnp.int32, (tq, tq), 1)).astype(BF16)

    def chunked(a):
        layers, rows, _ = a.shape
        return a.reshape(layers, rows, 2, nc, cw).transpose(0, 2, 3, 1, 4)

    mix_g, ffn_g = _row(mix_norm_g), _row(ffn_norm_g)
    def paired(a, b):
        tiles = [t.reshape(t.shape[0], t.shape[1], -1, LANES) for t in (a, b)]
        return jnp.stack(tiles, axis=3).reshape(a.shape[0], a.shape[1], -1)

    a_val, a_gate, b_gb, b_gc, b_h = jnp.split(conv_w_in.astype(BF16), 5, axis=-1)
    w_in = jnp.concatenate([paired(a_val, a_gate), b_gb, paired(b_gc, b_h)], axis=-1)
    w_out = conv_w_out.astype(BF16)
    a_dw_b, a_ln_g, a_ln_b = _row(conv_a_dw_b), _row(conv_a_ln_g), _row(conv_a_ln_b)
    w_qkv, w_o = attn_w_qkv.astype(BF16), attn_w_o.astype(BF16)
    q_g = _row(jnp.tile(attn_q_g * scale, (1, LANES // HEAD_DIM)))
    k_g = _row(jnp.tile(attn_k_g, (1, LANES // HEAD_DIM)))
    w_up = chunked(ffn_w_up.astype(BF16))
    dw_w = chunked(ffn_dw_w)
    dw_b = chunked(_row(ffn_dw_b))
    w_down = ffn_w_down.astype(BF16).reshape(depth, nc, cw, D)

    for layer in range(depth):
        i = layer // 2
        g = _Sel(mix_g, (layer,))
        mixer_out = {}
        if layer % 2 == 0:
            x = _conv_call(
                x, g, _Sel(w_in, (i,)), _Sel(conv_a_dw_w, (i,)), _Sel(a_dw_b, (i,)),
                _Sel(a_ln_g, (i,)), _Sel(a_ln_b, (i,)), _Sel(conv_b_dw_w, (i,)),
                _Sel(w_out, (i,)), tm=tm_conv)
        else:
            q, k, v = _qkv_call(x, g, _Sel(w_qkv, (i,)), _Sel(q_g, (i,)), _Sel(k_g, (i,)),
                                tm=tm)
            mixer_out = dict(attn=_attn_call(q, k, v, tri, tq=tq), wo=_Sel(w_o, (i,)))

        x = _ffn_call(
            x, _Sel(ffn_g, (layer,)),
            _Sel(w_up, (layer, 0)), _Sel(w_up, (layer, 1)),
            _Sel(dw_w, (layer, 0)), _Sel(dw_w, (layer, 1)),
            _Sel(dw_b, (layer, 0)), _Sel(dw_b, (layer, 1)),
            _Sel(w_down, (layer,)), tm=tm, **mixer_out)
    return x
```
